```python
import math
import jax
import jax.numpy as jnp
from jax import lax
import numpy as np

D_MODEL = 2048
BATCH = 4
SEQ = 8192
DEPTH = 2

MLA_HEADS = 8
MLA_NOPE_DIM = 128
MLA_ROPE_DIM = 64
MLA_V_DIM = 128
MLA_Q_LORA = D_MODEL // 4
MLA_KV_LORA = D_MODEL // 8
ROPE_BASE = 10000.0
Q_BLOCK = 128
GDN_HEADS = 8
GDN_HEAD_DIM = 128
GDN_WIDTH = GDN_HEADS * GDN_HEAD_DIM
GDN_CONV = 4
GDN_CHUNK = 64
LRU_WIDTH = D_MODEL // 2
LRU_BLOCKS = 8
LRU_BLOCK_DIM = LRU_WIDTH // LRU_BLOCKS
LRU_CONV = 4
LRU_C = 8.0
FFN_DIM = 5632
N_EXPERTS = 8
TOP_K = 2
MOE_BLOCK = 512
NORM_EPS = 1e-6
IN_WIDTHS = (MLA_Q_LORA, MLA_KV_LORA, MLA_ROPE_DIM, 3 * GDN_WIDTH, GDN_WIDTH, 4 * GDN_HEADS, LRU_WIDTH, LRU_WIDTH)
IN_WIDTH = sum(IN_WIDTHS)

kernel_name = 'bidir_hybrid_mla_gdn_rglru_moe_adaln'


def rms_norm(x, gain):
    xf = x.astype(jnp.float32)
    y = xf * lax.rsqrt(jnp.mean(xf * xf, axis=-1, keepdims=True) + NORM_EPS)
    return (y * gain.astype(jnp.float32)).astype(x.dtype)


def l2_normalize(x):
    return x * lax.rsqrt(jnp.sum(x * x, axis=-1, keepdims=True) + NORM_EPS)


def centred_depthwise_conv(x, w):
    K = w.shape[0]
    left = (K - 1) // 2
    S = x.shape[1]
    xp = jnp.pad(x, ((0, 0), (left, K - 1 - left), (0, 0)))
    return sum(w[k] * xp[:, k:k + S] for k in range(K))


def apply_rope(x, cos, sin):
    x1, x2 = jnp.split(x, 2, axis=-1)
    return jnp.concatenate([x1 * cos - x2 * sin, x1 * sin + x2 * cos], axis=-1).astype(x.dtype)


def mla_attention(c_q, c_kv, k_rope, cos, sin, q_norm, w_uq, kv_norm, w_ukv):
    B, S, _ = c_q.shape
    q = (rms_norm(c_q, q_norm) @ w_uq).reshape(B, S, MLA_HEADS, MLA_NOPE_DIM + MLA_ROPE_DIM)
    kv = (rms_norm(c_kv, kv_norm) @ w_ukv).reshape(B, S, MLA_HEADS, MLA_NOPE_DIM + MLA_V_DIM)
    q_nope, q_rope = q[..., :MLA_NOPE_DIM], q[..., MLA_NOPE_DIM:]
    k_nope, v = kv[..., :MLA_NOPE_DIM], kv[..., MLA_NOPE_DIM:]
    q_rope = apply_rope(q_rope, cos[:, :, None, :], sin[:, :, None, :])
    k_rope = apply_rope(k_rope, cos, sin)
    n_blk = S // Q_BLOCK
    to_blocks = lambda t: jnp.moveaxis(t.reshape((B, n_blk, Q_BLOCK) + t.shape[2:]), 1, 0)
    scale = (MLA_NOPE_DIM + MLA_ROPE_DIM) ** -0.5

    def attend(blk):
        qn, qr = blk
        s = jnp.einsum('bqhd,bkhd->bhqk', qn, k_nope) + jnp.einsum('bqhr,bkr->bhqk', qr, k_rope)
        p = jax.nn.softmax(s.astype(jnp.float32) * scale, axis=-1).astype(v.dtype)
        return jnp.einsum('bhqk,bkhd->bqhd', p, v)

    o = lax.map(attend, (to_blocks(q_nope), to_blocks(q_rope)))
    return jnp.moveaxis(o, 0, 1).reshape(B, S, MLA_HEADS * MLA_V_DIM)


def chunk_gated_delta_rule(q, k, v, log_alpha, beta):
    B, S, H, DK = q.shape
    DV = v.shape[-1]
    C = GDN_CHUNK
    N = S // C
    to_chunks = lambda t: jnp.moveaxis(t.reshape((B, N, C) + t.shape[2:]), 3, 1)
    q = to_chunks(q) * DK ** -0.5
    k, v, la, bt = to_chunks(k), to_chunks(v), to_chunks(log_alpha), to_chunks(beta)
    g = jnp.cumsum(la, axis=-1)
    idx = jnp.arange(C)
    incl = idx[:, None] >= idx[None, :]
    strict = idx[:, None] > idx[None, :]
    decay = jnp.exp(jnp.where(incl, g[..., :, None] - g[..., None, :], -jnp.inf))
    kb = k * bt[..., None]
    lower = jnp.where(strict, jnp.einsum('bhnid,bhnjd->bhnij', kb, k) * decay, 0.0)
    eye = jnp.eye(C, dtype=jnp.float32)
    rhs = jnp.concatenate([v * bt[..., None], kb * jnp.exp(g)[..., None]], axis=-1)
    sol = lax.linalg.triangular_solve(eye + lower, rhs, left_side=True, lower=True, unit_diagonal=True)
    u, w = sol[..., :DV], sol[..., DV:]
    qk = jnp.einsum('bhnid,bhnjd->bhnij', q, k) * decay
    q_dec = q * jnp.exp(g)[..., None]
    k_dec = k * jnp.exp(g[..., -1:] - g)[..., None]
    g_last = jnp.exp(g[..., -1])
    xs = tuple(jnp.moveaxis(t, 2, 0) for t in (u, w, qk, q_dec, k_dec, g_last))

    def step(state, inp):
        u_n, w_n, qk_n, qd_n, kd_n, gl_n = inp
        v_new = u_n - jnp.einsum('bhck,bhkv->bhcv', w_n, state)
        o_n = jnp.einsum('bhck,bhkv->bhcv', qd_n, state) + jnp.einsum('bhij,bhjv->bhiv', qk_n, v_new)
        state = state * gl_n[..., None, None] + jnp.einsum('bhck,bhcv->bhkv', kd_n, v_new)
        return state, o_n

    s0 = jnp.zeros((B, H, DK, DV), jnp.float32)
    _, o = lax.scan(step, s0, xs)
    return jnp.transpose(o, (1, 0, 3, 2, 4)).reshape(B, S, H, DV)


def gated_deltanet(qkv, z, ab, conv_w, a_log, dt_bias, out_norm):
    B, S, _ = qkv.shape
    f32 = jnp.float32
    qkv = jax.nn.silu(centred_depthwise_conv(qkv.astype(f32), conv_w.astype(f32)))
    q, k, v = jnp.split(qkv, 3, axis=-1)
    heads = lambda t: t.reshape(B, S, GDN_HEADS, GDN_HEAD_DIM)
    q, k, v = l2_normalize(heads(q)), l2_normalize(heads(k)), heads(v)
    a_f, b_f, a_b, b_b = jnp.split(ab.astype(f32), 4, axis=-1)
    a_log = a_log.astype(f32)
    dt_bias = dt_bias.astype(f32)
    log_alpha_f = -jnp.exp(a_log[0]) * jax.nn.softplus(a_f + dt_bias[0])
    log_alpha_b = -jnp.exp(a_log[1]) * jax.nn.softplus(a_b + dt_bias[1])
    o_f = chunk_gated_delta_rule(q, k, v, log_alpha_f, jax.nn.sigmoid(b_f))
    rev = lambda t: jnp.flip(t, axis=1)
    o_b = rev(chunk_gated_delta_rule(rev(q), rev(k), rev(v), rev(log_alpha_b), rev(jax.nn.sigmoid(b_b))))
    o = rms_norm(o_f + o_b, out_norm) * jax.nn.silu(heads(z.astype(f32)))
    return o.reshape(B, S, GDN_WIDTH)


def linear_recurrence_combine(left, right):
    a_l, b_l = left
    a_r, b_r = right
    return a_l * a_r, a_r * b_l + b_r


def rg_lru_branch(xr, y, conv_w, conv_b, w_a, b_a, w_i, b_i, lam):
    B, S, W = xr.shape
    f32 = jnp.float32
    xc = centred_depthwise_conv(xr.astype(f32), conv_w.astype(f32)) + conv_b.astype(f32)
    xb = xc.reshape(B, S, LRU_BLOCKS, LRU_BLOCK_DIM)

    def block_diag(wm, bv):
        return jnp.einsum('bsnd,nde->bsne', xb, wm.astype(f32)).reshape(B, S, W) + bv.astype(f32)

    def direction(d, reverse):
        r = jax.nn.sigmoid(block_diag(w_a[d], b_a[d]))
        i = jax.nn.sigmoid(block_diag(w_i[d], b_i[d]))
        log_a = -LRU_C * r * jax.nn.softplus(-lam[d].astype(f32))
        u = xc * i * jnp.sqrt(-jnp.expm1(2.0 * log_a))
        _, h = lax.associative_scan(linear_recurrence_combine, (jnp.exp(log_a), u), reverse=reverse, axis=1)
        return h

    h = direction(0, False) + direction(1, True)
    return h * jax.nn.gelu(y.astype(f32), approximate=True)


def swiglu(h, w1, w3, w2):
    return (jax.nn.silu(h @ w1) * (h @ w3)) @ w2


def moe_swiglu(h, w_router, b_router, w1, w3, w2):
    B, S, D = h.shape
    T = B * S
    hf = h.reshape(T, D)
    logits = (hf @ w_router).astype(jnp.float32) + b_router.astype(jnp.float32)
    top_logit, top_idx = lax.top_k(logits, TOP_K)
    top_w = jax.nn.softmax(top_logit, axis=-1)
    P = T * TOP_K
    flat_e = top_idx.reshape(P)
    flat_tok = jnp.repeat(jnp.arange(T, dtype=jnp.int32), TOP_K)
    order = jnp.argsort(flat_e)
    e_sorted = flat_e[order]
    counts = jnp.bincount(flat_e, length=N_EXPERTS)
    padded = (counts + MOE_BLOCK - 1) // MOE_BLOCK * MOE_BLOCK
    start = jnp.cumsum(counts) - counts
    pend = jnp.cumsum(padded)
    pstart = pend - padded
    dest = pstart[e_sorted] + jnp.arange(P, dtype=jnp.int32) - start[e_sorted]
    n_blocks = -(-P // MOE_BLOCK) + N_EXPERTS
    n_rows = n_blocks * MOE_BLOCK
    row_tok = jnp.zeros((n_rows,), jnp.int32).at[dest].set(flat_tok[order])
    row_w = jnp.zeros((n_rows,), jnp.float32).at[dest].set(top_w.reshape(P)[order])
    block_e = jnp.minimum(jnp.searchsorted(pend, jnp.arange(n_blocks) * MOE_BLOCK, side='right'), N_EXPERTS - 1)
    xs = hf[row_tok].reshape(n_blocks, MOE_BLOCK, D)

    def expert_block(args):
        xblk, e = args
        return swiglu(xblk, w1[e], w3[e], w2[e])

    y = lax.map(expert_block, (xs, block_e)).reshape(n_rows, D)
    y = y * row_w[:, None].astype(y.dtype)
    out = jnp.zeros((T, D), y.dtype).at[row_tok].add(y)
    return out.reshape(B, S, D)


def setup_inputs(seed: int = 0) -> dict:
    key = jax.random.key(seed)
    counter = [0]
    f32 = jnp.float32

    def nk():
        counter[0] += 1
        return jax.random.fold_in(key, counter[0])

    def normal(shape, scale):
        return jax.random.normal(nk(), shape, f32) * scale

    def gain(shape):
        return 1.0 + 0.02 * jax.random.normal(nk(), shape, f32)

    def uniform(shape, lo, hi):
        return jax.random.uniform(nk(), shape, f32, lo, hi)

    D, L = D_MODEL, DEPTH
    n_dense, n_moe = (DEPTH + 1) // 2, DEPTH // 2
    x = normal((BATCH, SEQ, D), 1.0)
    c = normal((BATCH, D), 1.0)
    offsets = jax.random.randint(nk(), (BATCH, 1), 0, 4096, dtype=jnp.int32)
    positions = offsets + jnp.arange(SEQ, dtype=jnp.int32)[None, :]
    dt = jnp.exp(uniform((L, 2, GDN_HEADS), math.log(1e-3), math.log(1e-1)))
    a_pow = uniform((L, 2, LRU_WIDTH), 0.9, 0.999) ** (1.0 / LRU_C)
    return {
        'x': x,
        'c': c,
        'positions': positions,
        'ada_w': normal((L, D, 6 * D), 0.5 * D ** -0.5),
        'ada_b': normal((L, 6 * D), 0.02),
        'norm_mix': gain((L, D)),
        'norm_ffn': gain((L, D)),
        'w_in': normal((L, D, IN_WIDTH), D ** -0.5),
        'mla_q_norm': gain((L, MLA_Q_LORA)),
        'mla_w_uq': normal((L, MLA_Q_LORA, MLA_HEADS * (MLA_NOPE_DIM + MLA_ROPE_DIM)), MLA_Q_LORA ** -0.5),
        'mla_kv_norm': gain((L, MLA_KV_LORA)),
        'mla_w_ukv': normal((L, MLA_KV_LORA, MLA_HEADS * (MLA_NOPE_DIM + MLA_V_DIM)), MLA_KV_LORA ** -0.5),
        'gdn_conv': normal((L, GDN_CONV, 3 * GDN_WIDTH), GDN_CONV ** -0.5),
        'gdn_a_log': jnp.log(uniform((L, 2, GDN_HEADS), 1.0, 16.0)),
        'gdn_dt_bias': dt + jnp.log(-jnp.expm1(-dt)),
        'gdn_out_norm': gain((L, GDN_HEAD_DIM)),
        'lru_conv_w': normal((L, LRU_CONV, LRU_WIDTH), LRU_CONV ** -0.5),
        'lru_conv_b': normal((L, LRU_WIDTH), 0.02),
        'lru_w_a': normal((L, 2, LRU_BLOCKS, LRU_BLOCK_DIM, LRU_BLOCK_DIM), LRU_BLOCK_DIM ** -0.5),
        'lru_b_a': normal((L, 2, LRU_WIDTH), 0.02),
        'lru_w_i': normal((L, 2, LRU_BLOCKS, LRU_BLOCK_DIM, LRU_BLOCK_DIM), LRU_BLOCK_DIM ** -0.5),
        'lru_b_i': normal((L, 2, LRU_WIDTH), 0.02),
        'lru_lambda': jnp.log(a_pow) - jnp.log1p(-a_pow),
        'w_branch_a': normal((L, MLA_HEADS * MLA_V_DIM, D), (MLA_HEADS * MLA_V_DIM) ** -0.5),
        'w_branch_b': normal((L, GDN_WIDTH, D), GDN_WIDTH ** -0.5),
        'w_branch_c': normal((L, LRU_WIDTH, D), LRU_WIDTH ** -0.5),
        'w_gate': normal((L, D, 3 * D), D ** -0.5),
        'w_out': normal((L, D, D), D ** -0.5),
        'ffn_w1': normal((n_dense, D, FFN_DIM), D ** -0.5),
        'ffn_w3': normal((n_dense, D, FFN_DIM), D ** -0.5),
        'ffn_w2': normal((n_dense, FFN_DIM, D), FFN_DIM ** -0.5),
        'moe_router': normal((n_moe, D, N_EXPERTS), D ** -0.5),
        'moe_router_b': normal((n_moe, N_EXPERTS), 0.01),
        'moe_w1': normal((n_moe, N_EXPERTS, D, FFN_DIM), D ** -0.5),
        'moe_w3': normal((n_moe, N_EXPERTS, D, FFN_DIM), D ** -0.5),
        'moe_w2': normal((n_moe, N_EXPERTS, FFN_DIM, D), FFN_DIM ** -0.5),
        'final_norm': gain((D,)),
    }


def reference(x, c, positions, ada_w, ada_b, norm_mix, norm_ffn, w_in, mla_q_norm, mla_w_uq, mla_kv_norm,
              mla_w_ukv, gdn_conv, gdn_a_log, gdn_dt_bias, gdn_out_norm, lru_conv_w, lru_conv_b, lru_w_a,
              lru_b_a, lru_w_i, lru_b_i, lru_lambda, w_branch_a, w_branch_b, w_branch_c, w_gate, w_out,
              ffn_w1, ffn_w3, ffn_w2, moe_router, moe_router_b, moe_w1, moe_w3, moe_w2, final_norm):
    inv_freq = ROPE_BASE ** (-jnp.arange(0, MLA_ROPE_DIM, 2, dtype=jnp.float32) / MLA_ROPE_DIM)
    ang = positions.astype(jnp.float32)[..., None] * inv_freq
    cos, sin = jnp.cos(ang), jnp.sin(ang)
    offsets = [int(o) for o in np.cumsum(IN_WIDTHS)[:-1]]
    cond = jax.nn.silu(c)
    for layer in range(DEPTH):
        mod = (cond @ ada_w[layer] + ada_b[layer])[:, None, :]
        shift_m, scale_m, gate_m, shift_f, scale_f, gate_f = jnp.split(mod, 6, axis=-1)
        h = rms_norm(x, norm_mix[layer]) * (1.0 + scale_m) + shift_m
        c_q, c_kv, k_rope, gdn_qkv, gdn_z, gdn_ab, lru_x, lru_y = jnp.split(h @ w_in[layer], offsets, axis=-1)
        o_a = mla_attention(c_q, c_kv, k_rope, cos, sin, mla_q_norm[layer], mla_w_uq[layer],
                            mla_kv_norm[layer], mla_w_ukv[layer])
        o_b = gated_deltanet(gdn_qkv, gdn_z, gdn_ab, gdn_conv[layer], gdn_a_log[layer], gdn_dt_bias[layer],
                             gdn_out_norm[layer]).astype(x.dtype)
        o_c = rg_lru_branch(lru_x, lru_y, lru_conv_w[layer], lru_conv_b[layer], lru_w_a[layer], lru_b_a[layer],
                            lru_w_i[layer], lru_b_i[layer], lru_lambda[layer]).astype(x.dtype)
        g_a, g_b, g_c = jnp.split(jax.nn.sigmoid(h @ w_gate[layer]), 3, axis=-1)
        merged = (g_a * (o_a @ w_branch_a[layer]) + g_b * (o_b @ w_branch_b[layer])
                  + g_c * (o_c @ w_branch_c[layer]))
        x = x + gate_m * (merged @ w_out[layer])
        h = rms_norm(x, norm_ffn[layer]) * (1.0 + scale_f) + shift_f
        j = layer // 2
        if layer % 2 == 0:
            y = swiglu(h, ffn_w1[j], ffn_w3[j], ffn_w2[j])
        else:
            y = moe_swiglu(h, moe_router[j], moe_router_b[j], moe_w1[j], moe_w3[j], moe_w2[j])
        x = x + gate_f * y
    return rms_norm(x, final_norm)
```

```python
import functools
import math

import jax
import jax.numpy as jnp
import numpy as np
from jax import lax
from jax.experimental import pallas as pl
from jax.experimental.pallas import tpu as pltpu

F32 = jnp.float32
BF16 = jnp.bfloat16
HIGHEST = lax.Precision.HIGHEST

LANES = 128
SUBLANES = 8
VMEM_LIMIT_BYTES = 48 * 1024 * 1024

NORM_EPS = 1e-6
ROPE_BASE = 10000.0
MLA_HEADS = 8
MLA_NOPE = 128
MLA_ROPE = 64
MLA_V = 128
MLA_QK_PAD = 256
GDN_HEADS = 8
GDN_DIM = 128
GDN_CHUNK = 64
LRU_BLOCKS = 8
LRU_C = 8.0
N_EXPERTS = 8
TOP_K = 2
MOE_BLOCK = 512
IN_FIXED_WIDTHS = (512, 256, MLA_ROPE, 3 * GDN_HEADS * GDN_DIM, GDN_HEADS * GDN_DIM, 4 * GDN_HEADS)
IN_FIXED_WIDTH = sum(IN_FIXED_WIDTHS)

P_GATE, P_QKV, P_Z, P_LRUX, P_LRUY, P_CQ, P_CKV, P_KROPE, P_AB, P_TOTAL = 0, 48, 72, 80, 88, 96, 100, 102, 103, 104


def _cparams(*semantics):
    return pltpu.CompilerParams(dimension_semantics=semantics, vmem_limit_bytes=VMEM_LIMIT_BYTES)


def _silu(x):
    return x * jax.nn.sigmoid(x)


def _softplus(x):
    return jnp.maximum(x, 0.0) + jnp.log1p(jnp.exp(-jnp.abs(x)))


def _bdot(a, b):
    return jnp.dot(a.astype(BF16), b.astype(BF16), preferred_element_type=F32)


def _bdot_nt(a, b):
    return lax.dot_general(a.astype(BF16), b.astype(BF16), (((1,), (1,)), ((), ())),
                           preferred_element_type=F32)


def _bdot_tn(a, b):
    return lax.dot_general(a.astype(BF16), b.astype(BF16), (((0,), (0,)), ((), ())),
                           preferred_element_type=F32)


def _fdot(a, b):
    return jnp.dot(a, b, preferred_element_type=F32, precision=HIGHEST)


def _ada_kernel(c_ref, w_ref, b_ref, o_ref):
    cond = _silu(c_ref[...])
    o_ref[...] = _bdot(cond, w_ref[...]) + b_ref[...]


def ada_modulation(c, ada_w, ada_b, *, tn=1024):
    n_layers, d, n = ada_w.shape
    bsz = c.shape[0]
    return pl.pallas_call(
        _ada_kernel,
        grid=(n_layers, n // tn),
        in_specs=[
            pl.BlockSpec((bsz, d), lambda l, j: (0, 0)),
            pl.BlockSpec((None, d, tn), lambda l, j: (l, 0, j)),
            pl.BlockSpec((None, 1, tn), lambda l, j: (l, 0, j)),
        ],
        out_specs=pl.BlockSpec((None, bsz, tn), lambda l, j: (l, 0, j)),
        out_shape=jax.ShapeDtypeStruct((n_layers, bsz, n), F32),
        compiler_params=_cparams("arbitrary", "arbitrary"),
        name="ada_modulation",
    )(c, ada_w, ada_b.reshape(n_layers, 1, n))


def _norm_mod(x, gain, scale, shift):
    y = x * lax.rsqrt(jnp.mean(x * x, axis=-1, keepdims=True) + NORM_EPS)
    return (y * gain) * (1.0 + scale) + shift


def _proj_kernel(x_ref, gain_ref, scale_ref, shift_ref, w_ref, o_ref, h_ref):
    @pl.when(pl.program_id(1) == 0)
    def _():
        h_ref[...] = _norm_mod(x_ref[...], gain_ref[...], scale_ref[...], shift_ref[...]).astype(BF16)

    o_ref[...] = jnp.dot(h_ref[...], w_ref[...], preferred_element_type=F32)


def _row_specs(d, tm, tiles_per_batch):
    return [
        pl.BlockSpec((tm, d), lambda i, j: (i, 0)),
        pl.BlockSpec((1, d), lambda i, j: (0, 0)),
        pl.BlockSpec((None, 1, d), lambda i, j: (i // tiles_per_batch, 0, 0)),
        pl.BlockSpec((None, 1, d), lambda i, j: (i // tiles_per_batch, 0, 0)),
    ]


def norm_projection(x, gain, scale, shift, w, seq, *, tm=512, tn=1024):
    t, d = x.shape
    n = w.shape[1]
    return pl.pallas_call(
        _proj_kernel,
        grid=(t // tm, n // tn),
        in_specs=_row_specs(d, tm, seq // tm) + [pl.BlockSpec((d, tn), lambda i, j: (0, j))],
        out_specs=pl.BlockSpec((tm, tn), lambda i, j: (i, j)),
        out_shape=jax.ShapeDtypeStruct((t, n), F32),
        scratch_shapes=[pltpu.VMEM((tm, d), BF16)],
        compiler_params=_cparams("arbitrary", "arbitrary"),
        name="norm_projection",
    )(x, gain.reshape(1, d), scale, shift, w)


def _rope_kernel(pos_ref, freq_ref, cos_ref, sin_ref):
    ang = pos_ref[...] * freq_ref[...]
    live = lax.broadcasted_iota(jnp.int32, ang.shape, 1) < MLA_ROPE
    cos_ref[...] = jnp.where(live, jnp.cos(ang), 0.0)
    sin_ref[...] = jnp.where(live, jnp.sin(ang), 0.0)


def rope_tables(positions, *, tm=1024):
    t = positions.size
    inv_freq = ROPE_BASE ** (-np.arange(0, MLA_ROPE, 2, dtype=np.float32) / MLA_ROPE)
    freq = np.zeros((1, LANES), np.float32)
    freq[0, :MLA_ROPE] = np.tile(inv_freq, 2)
    pos = positions.astype(F32).reshape(t, 1)
    return pl.pallas_call(
        _rope_kernel,
        grid=(t // tm,),
        in_specs=[pl.BlockSpec((tm, 1), lambda i: (i, 0)), pl.BlockSpec((1, LANES), lambda i: (0, 0))],
        out_specs=[pl.BlockSpec((tm, LANES), lambda i: (i, 0))] * 2,
        out_shape=[jax.ShapeDtypeStruct((t, LANES), F32)] * 2,
        compiler_params=_cparams("arbitrary"),
        name="rope_tables",
    )(pos, jnp.asarray(freq))


def _rope_half(t, cos, sin):
    lane = lax.broadcasted_iota(jnp.int32, t.shape, 1)
    rot = jnp.where(lane < MLA_ROPE // 2, -pltpu.roll(t, LANES - MLA_ROPE // 2, 1), pltpu.roll(t, MLA_ROPE // 2, 1))
    return t * cos + rot * sin


def _rms(x, gain):
    return x * lax.rsqrt(jnp.mean(x * x, axis=-1, keepdims=True) + NORM_EPS) * gain


def _mla_q_kernel(cq_ref, gain_ref, w_ref, cos_ref, sin_ref, o_ref):
    scale = (MLA_NOPE + MLA_ROPE) ** -0.5
    q = _bdot(_rms(cq_ref[...], gain_ref[...]), w_ref[...])
    cos, sin = cos_ref[...], sin_ref[...]
    for h in range(MLA_HEADS):
        lo = h * MLA_QK_PAD
        o_ref[:, lo:lo + LANES] = (q[:, lo:lo + LANES] * scale).astype(BF16)
        o_ref[:, lo + LANES:lo + 2 * LANES] = (_rope_half(q[:, lo + LANES:lo + 2 * LANES], cos, sin) * scale).astype(BF16)


def mla_q(p, gain, w_q, cos, sin, *, tm=512):
    t = p.shape[0]
    lora = gain.shape[0]
    n = w_q.shape[1]
    return pl.pallas_call(
        _mla_q_kernel,
        grid=(t // tm,),
        in_specs=[
            pl.BlockSpec((tm, lora), lambda i: (i, P_CQ * LANES // lora)),
            pl.BlockSpec((1, lora), lambda i: (0, 0)),
            pl.BlockSpec((lora, n), lambda i: (0, 0)),
            pl.BlockSpec((tm, LANES), lambda i: (i, 0)),
            pl.BlockSpec((tm, LANES), lambda i: (i, 0)),
        ],
        out_specs=pl.BlockSpec((tm, n), lambda i: (i, 0)),
        out_shape=jax.ShapeDtypeStruct((t, n), BF16),
        compiler_params=_cparams("arbitrary"),
        name="mla_q",
    )(p, gain.reshape(1, lora), w_q, cos, sin)


def _mla_kv_kernel(ckv_ref, kr_ref, gain_ref, wk_ref, wv_ref, cos_ref, sin_ref, k_ref, v_ref):
    n = _rms(ckv_ref[...], gain_ref[...]).astype(BF16)
    k_nope = jnp.dot(n, wk_ref[...], preferred_element_type=F32)
    v_ref[...] = jnp.dot(n, wv_ref[...], preferred_element_type=F32).astype(BF16)
    k_rope = _rope_half(kr_ref[...], cos_ref[...], sin_ref[...]).astype(BF16)
    for h in range(MLA_HEADS):
        lo = h * MLA_QK_PAD
        k_ref[:, lo:lo + LANES] = k_nope[:, h * MLA_NOPE:(h + 1) * MLA_NOPE].astype(BF16)
        k_ref[:, lo + LANES:lo + 2 * LANES] = k_rope


def mla_kv(p, gain, w_k, w_v, cos, sin, *, tm=512):
    t = p.shape[0]
    lora = gain.shape[0]
    return pl.pallas_call(
        _mla_kv_kernel,
        grid=(t // tm,),
        in_specs=[
            pl.BlockSpec((tm, lora), lambda i: (i, P_CKV * LANES // lora)),
            pl.BlockSpec((tm, LANES), lambda i: (i, P_KROPE)),
            pl.BlockSpec((1, lora), lambda i: (0, 0)),
            pl.BlockSpec(w_k.shape, lambda i: (0, 0)),
            pl.BlockSpec(w_v.shape, lambda i: (0, 0)),
            pl.BlockSpec((tm, LANES), lambda i: (i, 0)),
            pl.BlockSpec((tm, LANES), lambda i: (i, 0)),
        ],
        out_specs=[
            pl.BlockSpec((tm, MLA_HEADS * MLA_QK_PAD), lambda i: (i, 0)),
            pl.BlockSpec((tm, MLA_HEADS * MLA_V), lambda i: (i, 0)),
        ],
        out_shape=[
            jax.ShapeDtypeStruct((t, MLA_HEADS * MLA_QK_PAD), BF16),
            jax.ShapeDtypeStruct((t, MLA_HEADS * MLA_V), BF16),
        ],
        compiler_params=_cparams("arbitrary"),
        name="mla_kv",
    )(p, p, gain.reshape(1, lora), w_k, w_v, cos, sin)


def _flash_kernel(q_ref, k_ref, v_ref, o_ref, m_ref, l_ref, acc_ref):
    ki = pl.program_id(3)

    @pl.when(ki == 0)
    def _():
        m_ref[...] = jnp.full(m_ref.shape, -jnp.inf, F32)
        l_ref[...] = jnp.zeros(l_ref.shape, F32)
        acc_ref[...] = jnp.zeros(acc_ref.shape, F32)

    s = lax.dot_general(q_ref[...], k_ref[...], (((1,), (1,)), ((), ())), preferred_element_type=F32)
    m_prev = m_ref[...]
    m_new = jnp.maximum(m_prev, jnp.max(s, axis=-1, keepdims=True))
    alpha = jnp.exp(m_prev - m_new)
    p = jnp.exp(s - m_new)
    l_ref[...] = alpha * l_ref[...] + jnp.sum(p, axis=-1, keepdims=True)
    acc_ref[...] = alpha * acc_ref[...] + jnp.dot(p.astype(BF16), v_ref[...], preferred_element_type=F32)
    m_ref[...] = m_new

    @pl.when(ki == pl.num_programs(3) - 1)
    def _():
        o_ref[...] = (acc_ref[...] / l_ref[...]).astype(o_ref.dtype)


def flash_attention(q, k, v, bsz, seq, *, tq=512, tk=1024):
    t = q.shape[0]
    tq, tk = min(tq, seq), min(tk, seq)
    nq, nk = seq // tq, seq // tk
    return pl.pallas_call(
        _flash_kernel,
        grid=(bsz, MLA_HEADS, nq, nk),
        in_specs=[
            pl.BlockSpec((tq, MLA_QK_PAD), lambda b, h, qi, ki: (b * nq + qi, h)),
            pl.BlockSpec((tk, MLA_QK_PAD), lambda b, h, qi, ki: (b * nk + ki, h)),
            pl.BlockSpec((tk, MLA_V), lambda b, h, qi, ki: (b * nk + ki, h)),
        ],
        out_specs=pl.BlockSpec((tq, MLA_V), lambda b, h, qi, ki: (b * nq + qi, h)),
        out_shape=jax.ShapeDtypeStruct((t, MLA_HEADS * MLA_V), BF16),
        scratch_shapes=[pltpu.VMEM((tq, 1), F32), pltpu.VMEM((tq, 1), F32), pltpu.VMEM((tq, MLA_V), F32)],
        compiler_params=_cparams("arbitrary", "arbitrary", "arbitrary", "arbitrary"),
        name="flash_attention",
    )(q, k, v)


def _conv4(main_ref, prev_ref, next_ref, w_ref, pad_ref, first, last):
    tm = main_ref.shape[0]
    x = main_ref[...]
    pad_ref[0:SUBLANES, :] = jnp.where(first, 0.0, prev_ref[...])
    pad_ref[SUBLANES:SUBLANES + tm, :] = x
    pad_ref[SUBLANES + tm:2 * SUBLANES + tm, :] = jnp.where(last, 0.0, next_ref[...])
    w = w_ref[...]
    return (w[0:1] * pad_ref[SUBLANES - 1:SUBLANES - 1 + tm, :] + w[1:2] * x
            + w[2:3] * pad_ref[SUBLANES + 1:SUBLANES + 1 + tm, :]
            + w[3:4] * pad_ref[SUBLANES + 2:SUBLANES + 2 + tm, :])


def _halo_specs(tm, width, col, n_row8, grid_rank):
    r8 = tm // SUBLANES
    if grid_rank == 1:
        return [
            pl.BlockSpec((tm, width), lambda i: (i, col)),
            pl.BlockSpec((SUBLANES, width), lambda i: (jnp.maximum(i * r8 - 1, 0), col)),
            pl.BlockSpec((SUBLANES, width), lambda i: (jnp.minimum((i + 1) * r8, n_row8 - 1), col)),
        ]
    return [
        pl.BlockSpec((tm, width), lambda i, j: (i, col + j)),
        pl.BlockSpec((SUBLANES, width), lambda i, j: (jnp.maximum(i * r8 - 1, 0), col + j)),
        pl.BlockSpec((SUBLANES, width), lambda i, j: (jnp.minimum((i + 1) * r8, n_row8 - 1), col + j)),
    ]


def _gdn_prep_kernel(main_ref, prev_ref, next_ref, w_ref, o_ref, pad_ref, *, tiles_per_batch):
    i, j = pl.program_id(0), pl.program_id(1)
    first = (i % tiles_per_batch) == 0
    last = (i % tiles_per_batch) == tiles_per_batch - 1
    y = _silu(_conv4(main_ref, prev_ref, next_ref, w_ref, pad_ref, first, last))

    @pl.when(j < 2)
    def _():
        post = jnp.where(j == 0, GDN_DIM ** -0.5, 1.0)
        for h in range(GDN_HEADS):
            seg = y[:, h * GDN_DIM:(h + 1) * GDN_DIM]
            inv = lax.rsqrt(jnp.sum(seg * seg, axis=-1, keepdims=True) + NORM_EPS)
            o_ref[:, h * GDN_DIM:(h + 1) * GDN_DIM] = seg * (inv * post)

    @pl.when(j == 2)
    def _():
        o_ref[...] = y


def gdn_prep(p, conv_w, seq, *, tm=512):
    t = p.shape[0]
    width = GDN_HEADS * GDN_DIM
    col = P_QKV * LANES // width
    return pl.pallas_call(
        functools.partial(_gdn_prep_kernel, tiles_per_batch=seq // tm),
        grid=(t // tm, 3),
        in_specs=_halo_specs(tm, width, col, t // SUBLANES, 2) + [pl.BlockSpec((4, width), lambda i, j: (0, j))],
        out_specs=pl.BlockSpec((tm, width), lambda i, j: (i, j)),
        out_shape=jax.ShapeDtypeStruct((t, 3 * width), F32),
        scratch_shapes=[pltpu.VMEM((tm + 2 * SUBLANES, width), F32)],
        compiler_params=_cparams("arbitrary", "arbitrary"),
        name="gdn_prep",
    )(p, p, p, conv_w)


def _unit_triangular_inverse(low, eye):
    c = low.shape[0]
    inv = eye - low
    power = _fdot(low, low)
    span = 2
    while span < c:
        inv = inv + _fdot(inv, power)
        span *= 2
        if span < c:
            power = _fdot(power, power)
    return inv


def _gdn_chunk_kernel(q_ref, k_ref, v_ref, ab_ref, alog_ref, dt_ref, o_ref, state_ref, *, reverse):
    c = q_ref.shape[0]

    @pl.when(pl.program_id(1) == 0)
    def _():
        state_ref[...] = jnp.zeros(state_ref.shape, F32)

    row = lax.broadcasted_iota(jnp.int32, (c, c), 0)
    col = lax.broadcasted_iota(jnp.int32, (c, c), 1)
    if reverse:
        incl, strict, cum = row <= col, row < col, (col >= row).astype(F32)
    else:
        incl, strict, cum = row >= col, row > col, (col <= row).astype(F32)
    eye = (row == col).astype(F32)
    eye_l = (lax.broadcasted_iota(jnp.int32, (LANES, LANES), 0)
             == lax.broadcasted_iota(jnp.int32, (LANES, LANES), 1)).astype(F32)

    ab = ab_ref[...]
    log_alpha = -jnp.exp(alog_ref[...]) * _softplus(ab + dt_ref[...])
    beta_all = jax.nn.sigmoid(ab)
    g_all = _fdot(cum, log_alpha)
    g_rows = lax.dot_general(eye_l, g_all, (((1,), (1,)), ((), ())), preferred_element_type=F32,
                             precision=HIGHEST)
    a_lane = 2 * GDN_HEADS if reverse else 0
    last = 0 if reverse else c - 1

    for h in range(GDN_HEADS):
        sl = slice(h * GDN_DIM, (h + 1) * GDN_DIM)
        q, k, v = q_ref[:, sl], k_ref[:, sl], v_ref[:, sl]
        g_col = g_all[:, a_lane + h:a_lane + h + 1]
        g_row = g_rows[a_lane + h:a_lane + h + 1, :]
        beta = beta_all[:, a_lane + GDN_HEADS + h:a_lane + GDN_HEADS + h + 1]
        decay = jnp.exp(jnp.where(incl, g_col - g_row, -jnp.inf))
        kb = k * beta
        low = jnp.where(strict, _bdot_nt(kb, k) * decay, 0.0)
        inv = _unit_triangular_inverse(low, eye)
        eg = jnp.exp(g_col)
        sol = _bdot(inv, jnp.concatenate([v * beta, kb * eg], axis=1))
        u, w = sol[:, :GDN_DIM], sol[:, GDN_DIM:]
        qk = _bdot_nt(q, k) * decay
        g_last = g_col[last:last + 1, :]
        state = state_ref[h]
        v_new = u - _bdot(w, state)
        o_ref[:, sl] = _bdot(q * eg, state) + _bdot(qk, v_new)
        state_ref[h] = state * jnp.exp(g_last) + _bdot_tn(k * jnp.exp(g_last - g_col), v_new)


def gdn_chunk_scan(qkv, p, a_log_row, dt_row, bsz, seq, *, reverse):
    t = qkv.shape[0]
    c = GDN_CHUNK
    n = seq // c
    width = GDN_HEADS * GDN_DIM

    def rows(b, i):
        return b * n + (n - 1 - i if reverse else i)

    return pl.pallas_call(
        functools.partial(_gdn_chunk_kernel, reverse=reverse),
        grid=(bsz, n),
        in_specs=[
            pl.BlockSpec((c, width), lambda b, i: (rows(b, i), 0)),
            pl.BlockSpec((c, width), lambda b, i: (rows(b, i), 1)),
            pl.BlockSpec((c, width), lambda b, i: (rows(b, i), 2)),
            pl.BlockSpec((c, LANES), lambda b, i: (rows(b, i), P_AB)),
            pl.BlockSpec((1, LANES), lambda b, i: (0, 0)),
            pl.BlockSpec((1, LANES), lambda b, i: (0, 0)),
        ],
        out_specs=pl.BlockSpec((c, width), lambda b, i: (rows(b, i), 0)),
        out_shape=jax.ShapeDtypeStruct((t, width), F32),
        scratch_shapes=[pltpu.VMEM((GDN_HEADS, GDN_DIM, GDN_DIM), F32)],
        compiler_params=_cparams("arbitrary", "arbitrary"),
        name="gdn_chunk_rev" if reverse else "gdn_chunk_fwd",
    )(qkv, qkv, qkv, p, a_log_row, dt_row)


def _gdn_out_kernel(of_ref, ob_ref, z_ref, gain_ref, o_ref):
    o = of_ref[...] + ob_ref[...]
    z = z_ref[...]
    gain = gain_ref[...]
    for h in range(GDN_HEADS):
        sl = slice(h * GDN_DIM, (h + 1) * GDN_DIM)
        o_ref[:, sl] = (_rms(o[:, sl], gain) * _silu(z[:, sl])).astype(BF16)


def gdn_output(o_f, o_b, p, out_norm, *, tm=512):
    t, width = o_f.shape
    return pl.pallas_call(
        _gdn_out_kernel,
        grid=(t // tm,),
        in_specs=[
            pl.BlockSpec((tm, width), lambda i: (i, 0)),
            pl.BlockSpec((tm, width), lambda i: (i, 0)),
            pl.BlockSpec((tm, width), lambda i: (i, P_Z * LANES // width)),
            pl.BlockSpec((1, GDN_DIM), lambda i: (0, 0)),
        ],
        out_specs=pl.BlockSpec((tm, width), lambda i: (i, 0)),
        out_shape=jax.ShapeDtypeStruct((t, width), BF16),
        compiler_params=_cparams("arbitrary"),
        name="gdn_output",
    )(o_f, o_b, p, out_norm.reshape(1, GDN_DIM))


def _lru_scan_tile(a, u, carry_ref, a_ref, b_ref, h_ref, reverse):
    tm = a.shape[0]
    sub = lax.broadcasted_iota(jnp.int32, a.shape, 0) % SUBLANES
    shift = 1
    while shift < SUBLANES:
        if reverse:
            ok = sub < SUBLANES - shift
            a_sh, u_sh = pltpu.roll(a, tm - shift, 0), pltpu.roll(u, tm - shift, 0)
        else:
            ok = sub >= shift
            a_sh, u_sh = pltpu.roll(a, shift, 0), pltpu.roll(u, shift, 0)
        u = a * jnp.where(ok, u_sh, 0.0) + u
        a = a * jnp.where(ok, a_sh, 1.0)
        shift *= 2
    a_ref[...] = a
    b_ref[...] = u
    groups = tm // SUBLANES
    edge = 0 if reverse else SUBLANES - 1

    def body(g, h_in):
        r0 = pl.multiple_of((groups - 1 - g if reverse else g) * SUBLANES, SUBLANES)
        h = a_ref[pl.ds(r0, SUBLANES), :] * h_in + b_ref[pl.ds(r0, SUBLANES), :]
        h_ref[pl.ds(r0, SUBLANES), :] = h
        return jnp.broadcast_to(h[edge:edge + 1, :], h.shape)

    carry_ref[...] = lax.fori_loop(0, groups, body, carry_ref[...], unroll=4)


def _lru_gates(xc, w_ref, ba_ref, bi_ref, lam_ref):
    blk = xc.shape[1] // LRU_BLOCKS
    parts = [_bdot(xc[:, n * blk:(n + 1) * blk], w_ref[n]) for n in range(LRU_BLOCKS)]
    r = jax.nn.sigmoid(jnp.concatenate([p[:, :blk] for p in parts], axis=1) + ba_ref[...])
    gate_i = jax.nn.sigmoid(jnp.concatenate([p[:, blk:] for p in parts], axis=1) + bi_ref[...])
    log_a = -LRU_C * r * _softplus(-lam_ref[...])
    a = jnp.exp(log_a)
    return a, xc * gate_i * jnp.sqrt(1.0 - a * a)


def _lru_fwd_kernel(main_ref, prev_ref, next_ref, cw_ref, cb_ref, w_ref, ba_ref, bi_ref, lam_ref,
                    h_ref, pad_ref, carry_ref, a_ref, b_ref, *, tiles_per_batch):
    i = pl.program_id(0) % tiles_per_batch

    @pl.when(i == 0)
    def _():
        carry_ref[...] = jnp.zeros(carry_ref.shape, F32)

    xc = _conv4(main_ref, prev_ref, next_ref, cw_ref, pad_ref, i == 0, i == tiles_per_batch - 1) + cb_ref[...]
    a, u = _lru_gates(xc, w_ref, ba_ref, bi_ref, lam_ref)
    _lru_scan_tile(a, u, carry_ref, a_ref, b_ref, h_ref, False)


def _lru_bwd_kernel(main_ref, prev_ref, next_ref, cw_ref, cb_ref, w_ref, ba_ref, bi_ref, lam_ref, hf_ref, y_ref,
                    o_ref, pad_ref, carry_ref, a_ref, b_ref, h_ref, *, tiles_per_batch):
    i = pl.program_id(0) % tiles_per_batch

    @pl.when(i == 0)
    def _():
        carry_ref[...] = jnp.zeros(carry_ref.shape, F32)

    xc = _conv4(main_ref, prev_ref, next_ref, cw_ref, pad_ref, i == tiles_per_batch - 1, i == 0) + cb_ref[...]
    a, u = _lru_gates(xc, w_ref, ba_ref, bi_ref, lam_ref)
    _lru_scan_tile(a, u, carry_ref, a_ref, b_ref, h_ref, True)
    o_ref[...] = ((hf_ref[...] + h_ref[...]) * jax.nn.gelu(y_ref[...], approximate=True)).astype(BF16)


def _lru_param_specs(width, blk):
    return [
        pl.BlockSpec((4, width), lambda i: (0, 0)),
        pl.BlockSpec((1, width), lambda i: (0, 0)),
        pl.BlockSpec((LRU_BLOCKS, blk, 2 * blk), lambda i: (0, 0, 0)),
        pl.BlockSpec((1, width), lambda i: (0, 0)),
        pl.BlockSpec((1, width), lambda i: (0, 0)),
        pl.BlockSpec((1, width), lambda i: (0, 0)),
    ]


def _lru_scratch(tm, width):
    return [pltpu.VMEM((tm + 2 * SUBLANES, width), F32), pltpu.VMEM((SUBLANES, width), F32),
            pltpu.VMEM((tm, width), F32), pltpu.VMEM((tm, width), F32)]


def rg_lru(p, conv_w, conv_b, w_a, b_a, w_i, b_i, lam, seq, *, tm=256):
    t = p.shape[0]
    width = conv_w.shape[1]
    blk = width // LRU_BLOCKS
    n_tiles = t // tm
    col_x, col_y = P_LRUX * LANES // width, P_LRUY * LANES // width
    w_dir = [jnp.concatenate([w_a[d], w_i[d]], axis=-1).astype(BF16) for d in range(2)]
    params = lambda d: (conv_w, conv_b.reshape(1, width), w_dir[d], b_a[d].reshape(1, width),
                        b_i[d].reshape(1, width), lam[d].reshape(1, width))
    h_f = pl.pallas_call(
        functools.partial(_lru_fwd_kernel, tiles_per_batch=seq // tm),
        grid=(n_tiles,),
        in_specs=_halo_specs(tm, width, col_x, t // SUBLANES, 1) + _lru_param_specs(width, blk),
        out_specs=pl.BlockSpec((tm, width), lambda i: (i, 0)),
        out_shape=jax.ShapeDtypeStruct((t, width), F32),
        scratch_shapes=_lru_scratch(tm, width),
        compiler_params=_cparams("arbitrary"),
        name="rg_lru_fwd",
    )(p, p, p, *params(0))
    r8 = tm // SUBLANES
    n_row8 = t // SUBLANES
    rev = lambda i: n_tiles - 1 - i
    halo_rev = [
        pl.BlockSpec((tm, width), lambda i: (rev(i), col_x)),
        pl.BlockSpec((SUBLANES, width), lambda i: (jnp.maximum(rev(i) * r8 - 1, 0), col_x)),
        pl.BlockSpec((SUBLANES, width), lambda i: (jnp.minimum((rev(i) + 1) * r8, n_row8 - 1), col_x)),
    ]
    return pl.pallas_call(
        functools.partial(_lru_bwd_kernel, tiles_per_batch=seq // tm),
        grid=(n_tiles,),
        in_specs=halo_rev + _lru_param_specs(width, blk) + [
            pl.BlockSpec((tm, width), lambda i: (rev(i), 0)),
            pl.BlockSpec((tm, width), lambda i: (rev(i), col_y)),
        ],
        out_specs=pl.BlockSpec((tm, width), lambda i: (rev(i), 0)),
        out_shape=jax.ShapeDtypeStruct((t, width), BF16),
        scratch_shapes=_lru_scratch(tm, width) + [pltpu.VMEM((tm, width), F32)],
        compiler_params=_cparams("arbitrary"),
        name="rg_lru_bwd",
    )(p, p, p, *params(1), h_f, p)


def _merge_kernel(oa_ref, ob_ref, oc_ref, wa_ref, wb_ref, wc_ref, ga_ref, gb_ref, gc_ref, o_ref):
    acc = jax.nn.sigmoid(ga_ref[...]) * jnp.dot(oa_ref[...], wa_ref[...], preferred_element_type=F32)
    acc += jax.nn.sigmoid(gb_ref[...]) * jnp.dot(ob_ref[...], wb_ref[...], preferred_element_type=F32)
    acc += jax.nn.sigmoid(gc_ref[...]) * jnp.dot(oc_ref[...], wc_ref[...], preferred_element_type=F32)
    o_ref[...] = acc.astype(BF16)


def merge_branches(o_a, o_b, o_c, w_a, w_b, w_c, p, *, tm=512, tn=512):
    t = o_a.shape[0]
    d = w_a.shape[1]
    n_col = d // tn
    branch = lambda arr: pl.BlockSpec((tm, arr.shape[1]), lambda j, i: (i, 0))
    weight = lambda arr: pl.BlockSpec((arr.shape[0], tn), lambda j, i: (0, j))
    gate = lambda g: pl.BlockSpec((tm, tn), lambda j, i: (i, P_GATE * LANES // tn + g * n_col + j))
    return pl.pallas_call(
        _merge_kernel,
        grid=(n_col, t // tm),
        in_specs=[branch(o_a), branch(o_b), branch(o_c), weight(w_a), weight(w_b), weight(w_c),
                  gate(0), gate(1), gate(2)],
        out_specs=pl.BlockSpec((tm, tn), lambda j, i: (i, j)),
        out_shape=jax.ShapeDtypeStruct((t, d), BF16),
        compiler_params=_cparams("arbitrary", "arbitrary"),
        name="merge_branches",
    )(o_a, o_b, o_c, w_a, w_b, w_c, p, p, p)


def _residual_kernel(a_ref, w_ref, x_ref, g_ref, o_ref):
    o_ref[...] = x_ref[...] + g_ref[...] * jnp.dot(a_ref[...], w_ref[...], preferred_element_type=F32)


def matmul_gated_residual(a, w, x, gate, seq, *, tm=512, tn=512):
    t, k = a.shape
    d = w.shape[1]
    tiles_per_batch = seq // tm
    return pl.pallas_call(
        _residual_kernel,
        grid=(t // tm, d // tn),
        in_specs=[
            pl.BlockSpec((tm, k), lambda i, j: (i, 0)),
            pl.BlockSpec((k, tn), lambda i, j: (0, j)),
            pl.BlockSpec((tm, tn), lambda i, j: (i, j)),
            pl.BlockSpec((None, 1, tn), lambda i, j: (i // tiles_per_batch, 0, j)),
        ],
        out_specs=pl.BlockSpec((tm, tn), lambda i, j: (i, j)),
        out_shape=jax.ShapeDtypeStruct((t, d), F32),
        compiler_params=_cparams("arbitrary", "arbitrary"),
        name="matmul_gated_residual",
    )(a, w, x, gate)


def _swiglu_up_kernel(x_ref, gain_ref, scale_ref, shift_ref, w1_ref, w3_ref, o_ref, h_ref):
    @pl.when(pl.program_id(1) == 0)
    def _():
        h_ref[...] = _norm_mod(x_ref[...], gain_ref[...], scale_ref[...], shift_ref[...]).astype(BF16)

    h = h_ref[...]
    up = jnp.dot(h, w1_ref[...], preferred_element_type=F32)
    o_ref[...] = (_silu(up) * jnp.dot(h, w3_ref[...], preferred_element_type=F32)).astype(BF16)


def norm_swiglu_up(x, gain, scale, shift, w1, w3, seq, *, tm=512, tn=512):
    t, d = x.shape
    f = w1.shape[1]
    return pl.pallas_call(
        _swiglu_up_kernel,
        grid=(t // tm, f // tn),
        in_specs=_row_specs(d, tm, seq // tm) + [pl.BlockSpec((d, tn), lambda i, j: (0, j))] * 2,
        out_specs=pl.BlockSpec((tm, tn), lambda i, j: (i, j)),
        out_shape=jax.ShapeDtypeStruct((t, f), BF16),
        scratch_shapes=[pltpu.VMEM((tm, d), BF16)],
        compiler_params=_cparams("arbitrary", "arbitrary"),
        name="norm_swiglu_up",
    )(x, gain.reshape(1, d), scale, shift, w1, w3)


def _router_kernel(x_ref, gain_ref, scale_ref, shift_ref, w_ref, b_ref, h_ref, route_ref):
    h = _norm_mod(x_ref[...], gain_ref[...], scale_ref[...], shift_ref[...])
    h_ref[...] = h.astype(BF16)
    lane = lax.broadcasted_iota(jnp.int32, route_ref.shape, 1)
    logits = jnp.where(lane < N_EXPERTS, _fdot(h, w_ref[...]) + b_ref[...], -jnp.inf)
    top1 = jnp.max(logits, axis=-1, keepdims=True)
    idx1 = jnp.min(jnp.where(logits == top1, lane, LANES), axis=-1, keepdims=True)
    rest = jnp.where(lane == idx1, -jnp.inf, logits)
    top2 = jnp.max(rest, axis=-1, keepdims=True)
    idx2 = jnp.min(jnp.where(rest == top2, lane, LANES), axis=-1, keepdims=True)
    e2 = jnp.exp(top2 - top1)
    inv = 1.0 / (1.0 + e2)
    route_ref[...] = jnp.where(lane == 0, idx1.astype(F32),
                               jnp.where(lane == 1, idx2.astype(F32),
                                         jnp.where(lane == 2, inv, jnp.where(lane == 3, e2 * inv, 0.0))))


def norm_router(x, gain, scale, shift, w_router, b_router, seq, *, tm=512):
    t, d = x.shape
    w_pad = jnp.zeros((d, LANES), F32).at[:, :N_EXPERTS].set(w_router)
    b_pad = jnp.zeros((1, LANES), F32).at[0, :N_EXPERTS].set(b_router)
    tiles_per_batch = seq // tm
    return pl.pallas_call(
        _router_kernel,
        grid=(t // tm,),
        in_specs=[
            pl.BlockSpec((tm, d), lambda i: (i, 0)),
            pl.BlockSpec((1, d), lambda i: (0, 0)),
            pl.BlockSpec((None, 1, d), lambda i: (i // tiles_per_batch, 0, 0)),
            pl.BlockSpec((None, 1, d), lambda i: (i // tiles_per_batch, 0, 0)),
            pl.BlockSpec((d, LANES), lambda i: (0, 0)),
            pl.BlockSpec((1, LANES), lambda i: (0, 0)),
        ],
        out_specs=[pl.BlockSpec((tm, d), lambda i: (i, 0)), pl.BlockSpec((tm, LANES), lambda i: (i, 0))],
        out_shape=[jax.ShapeDtypeStruct((t, d), BF16), jax.ShapeDtypeStruct((t, LANES), F32)],
        compiler_params=_cparams("arbitrary"),
        name="norm_router",
    )(x, gain.reshape(1, d), scale, shift, w_pad, b_pad)


def _expert_up_kernel(be_ref, x_ref, w1_ref, w3_ref, o_ref):
    x = x_ref[...]
    up = jnp.dot(x, w1_ref[...], preferred_element_type=F32)
    o_ref[...] = (_silu(up) * jnp.dot(x, w3_ref[...], preferred_element_type=F32)).astype(BF16)


def expert_swiglu_up(block_e, xs, w1, w3, *, tn=512):
    rows, d = xs.shape
    f = w1.shape[2]
    return pl.pallas_call(
        _expert_up_kernel,
        grid_spec=pltpu.PrefetchScalarGridSpec(
            num_scalar_prefetch=1,
            grid=(f // tn, rows // MOE_BLOCK),
            in_specs=[
                pl.BlockSpec((MOE_BLOCK, d), lambda j, i, be: (i, 0)),
                pl.BlockSpec((None, d, tn), lambda j, i, be: (be[i], 0, j)),
                pl.BlockSpec((None, d, tn), lambda j, i, be: (be[i], 0, j)),
            ],
            out_specs=pl.BlockSpec((MOE_BLOCK, tn), lambda j, i, be: (i, j)),
        ),
        out_shape=jax.ShapeDtypeStruct((rows, f), BF16),
        compiler_params=_cparams("arbitrary", "arbitrary"),
        name="expert_swiglu_up",
    )(block_e, xs, w1, w3)


def _expert_down_kernel(be_ref, a_ref, w_ref, o_ref):
    o_ref[...] = jnp.dot(a_ref[...], w_ref[...], preferred_element_type=F32)


def expert_down(block_e, act, w2, *, tn=512):
    rows, f = act.shape
    d = w2.shape[2]
    return pl.pallas_call(
        _expert_down_kernel,
        grid_spec=pltpu.PrefetchScalarGridSpec(
            num_scalar_prefetch=1,
            grid=(d // tn, rows // MOE_BLOCK),
            in_specs=[
                pl.BlockSpec((MOE_BLOCK, f), lambda j, i, be: (i, 0)),
                pl.BlockSpec((None, f, tn), lambda j, i, be: (be[i], 0, j)),
            ],
            out_specs=pl.BlockSpec((MOE_BLOCK, tn), lambda j, i, be: (i, j)),
        ),
        out_shape=jax.ShapeDtypeStruct((rows, d), F32),
        compiler_params=_cparams("arbitrary", "arbitrary"),
        name="expert_down",
    )(block_e, act, w2)


def _combine_kernel(x_ref, y1_ref, y2_ref, route_ref, g_ref, o_ref):
    route = route_ref[...]
    y = route[:, 2:3] * y1_ref[...] + route[:, 3:4] * y2_ref[...]
    o_ref[...] = x_ref[...] + g_ref[...] * y


def combine_residual(x, y1, y2, route, gate, seq, *, tm=512):
    t, d = x.shape
    tiles_per_batch = seq // tm
    row = pl.BlockSpec((tm, d), lambda i: (i, 0))
    return pl.pallas_call(
        _combine_kernel,
        grid=(t // tm,),
        in_specs=[row, row, row, pl.BlockSpec((tm, LANES), lambda i: (i, 0)),
                  pl.BlockSpec((None, 1, d), lambda i: (i // tiles_per_batch, 0, 0))],
        out_specs=row,
        out_shape=jax.ShapeDtypeStruct((t, d), F32),
        compiler_params=_cparams("arbitrary"),
        name="combine_residual",
    )(x, y1, y2, route, gate)


def routing_plan(route, n_tokens):
    n_pairs = n_tokens * TOP_K
    flat_e = route[:, :TOP_K].astype(jnp.int32).reshape(n_pairs)
    onehot = (flat_e[:, None] == jnp.arange(N_EXPERTS, dtype=jnp.int32)[None, :]).astype(jnp.int32)
    running = jnp.cumsum(onehot, axis=0)
    counts = running[-1]
    rank = jnp.take_along_axis(running, flat_e[:, None], axis=1)[:, 0] - 1
    padded = (counts + MOE_BLOCK - 1) // MOE_BLOCK * MOE_BLOCK
    pend = jnp.cumsum(padded)
    dest = (pend - padded)[flat_e] + rank
    n_blocks = -(-n_pairs // MOE_BLOCK) + N_EXPERTS
    flat_tok = jnp.repeat(jnp.arange(n_tokens, dtype=jnp.int32), TOP_K)
    row_tok = jnp.zeros((n_blocks * MOE_BLOCK,), jnp.int32).at[dest].set(flat_tok)
    block_e = jnp.minimum(jnp.searchsorted(pend, jnp.arange(n_blocks, dtype=jnp.int32) * MOE_BLOCK, side="right"),
                          N_EXPERTS - 1).astype(jnp.int32)
    return row_tok, dest.reshape(n_tokens, TOP_K), block_e


def _final_norm_kernel(x_ref, gain_ref, o_ref):
    o_ref[...] = _rms(x_ref[...], gain_ref[...])


def final_rms_norm(x, gain, *, tm=512):
    t, d = x.shape
    return pl.pallas_call(
        _final_norm_kernel,
        grid=(t // tm,),
        in_specs=[pl.BlockSpec((tm, d), lambda i: (i, 0)), pl.BlockSpec((1, d), lambda i: (0, 0))],
        out_specs=pl.BlockSpec((tm, d), lambda i: (i, 0)),
        out_shape=jax.ShapeDtypeStruct((t, d), F32),
        compiler_params=_cparams("arbitrary"),
        name="final_norm",
    )(x, gain.reshape(1, d))


def _pad_cols(w, width):
    return jnp.pad(w, ((0, 0), (0, width - w.shape[1])))


def projection_weight(w_in, w_gate):
    lru_width = (w_in.shape[1] - IN_FIXED_WIDTH) // 2
    o = np.cumsum((0,) + IN_FIXED_WIDTHS + (lru_width, lru_width))
    c_q, c_kv, k_rope, qkv, z, ab, lru_x, lru_y = (w_in[:, int(o[k]):int(o[k + 1])] for k in range(8))
    parts = [w_gate, qkv, z, lru_x, lru_y, c_q, c_kv, _pad_cols(k_rope, LANES), _pad_cols(ab, LANES)]
    return jnp.concatenate(parts, axis=1).astype(BF16)


def mla_weights(w_uq, w_ukv):
    d_q, d_kv = w_uq.shape[0], w_ukv.shape[0]
    wq = w_uq.reshape(d_q, MLA_HEADS, MLA_NOPE + MLA_ROPE)
    wq = jnp.pad(wq, ((0, 0), (0, 0), (0, MLA_QK_PAD - MLA_NOPE - MLA_ROPE))).reshape(d_q, MLA_HEADS * MLA_QK_PAD)
    wkv = w_ukv.reshape(d_kv, MLA_HEADS, MLA_NOPE + MLA_V)
    wk = wkv[:, :, :MLA_NOPE].reshape(d_kv, MLA_HEADS * MLA_NOPE)
    wv = wkv[:, :, MLA_NOPE:].reshape(d_kv, MLA_HEADS * MLA_V)
    return wq.astype(BF16), wk.astype(BF16), wv.astype(BF16)


def _gdn_gate_rows(a_log, dt_bias):
    zeros = jnp.zeros((GDN_HEADS,), F32)
    row = lambda v: _pad_cols(jnp.concatenate([v[0], zeros, v[1], zeros])[None, :], LANES)
    return row(a_log.astype(F32)), row(dt_bias.astype(F32))


def kernel(x, c, positions, ada_w, ada_b, norm_mix, norm_ffn, w_in, mla_q_norm, mla_w_uq, mla_kv_norm,
           mla_w_ukv, gdn_conv, gdn_a_log, gdn_dt_bias, gdn_out_norm, lru_conv_w, lru_conv_b, lru_w_a,
           lru_b_a, lru_w_i, lru_b_i, lru_lambda, w_branch_a, w_branch_b, w_branch_c, w_gate, w_out,
           ffn_w1, ffn_w3, ffn_w2, moe_router, moe_router_b, moe_w1, moe_w3, moe_w2, final_norm):
    bsz, seq, d = x.shape
    t = bsz * seq
    depth = ada_w.shape[0]
    xf = x.reshape(t, d)
    mod = ada_modulation(c, ada_w, ada_b).reshape(depth, bsz, 6, 1, d)
    cos, sin = rope_tables(positions)
    for layer in range(depth):
        shift_m, scale_m, gate_m, shift_f, scale_f, gate_f = (mod[layer, :, k] for k in range(6))
        p = norm_projection(xf, norm_mix[layer], scale_m, shift_m, projection_weight(w_in[layer], w_gate[layer]), seq)
        wq, wk, wv = mla_weights(mla_w_uq[layer], mla_w_ukv[layer])
        q = mla_q(p, mla_q_norm[layer], wq, cos, sin)
        k, v = mla_kv(p, mla_kv_norm[layer], wk, wv, cos, sin)
        o_a = flash_attention(q, k, v, bsz, seq)
        qkv = gdn_prep(p, gdn_conv[layer], seq)
        a_log_row, dt_row = _gdn_gate_rows(gdn_a_log[layer], gdn_dt_bias[layer])
        o_f = gdn_chunk_scan(qkv, p, a_log_row, dt_row, bsz, seq, reverse=False)
        o_r = gdn_chunk_scan(qkv, p, a_log_row, dt_row, bsz, seq, reverse=True)
        o_b = gdn_output(o_f, o_r, p, gdn_out_norm[layer])
        o_c = rg_lru(p, lru_conv_w[layer], lru_conv_b[layer], lru_w_a[layer], lru_b_a[layer], lru_w_i[layer],
                     lru_b_i[layer], lru_lambda[layer], seq)
        merged = merge_branches(o_a, o_b, o_c, w_branch_a[layer].astype(BF16), w_branch_b[layer].astype(BF16),
                                w_branch_c[layer].astype(BF16), p)
        xf = matmul_gated_residual(merged, w_out[layer].astype(BF16), xf, gate_m, seq)
        j = layer // 2
        if layer % 2 == 0:
            act = norm_swiglu_up(xf, norm_ffn[layer], scale_f, shift_f, ffn_w1[j].astype(BF16),
                                 ffn_w3[j].astype(BF16), seq)
            xf = matmul_gated_residual(act, ffn_w2[j].astype(BF16), xf, gate_f, seq)
        else:
            h, route = norm_router(xf, norm_ffn[layer], scale_f, shift_f, moe_router[j], moe_router_b[j], seq)
            row_tok, dest, block_e = routing_plan(route, t)
            act = expert_swiglu_up(block_e, jnp.take(h, row_tok, axis=0), moe_w1[j].astype(BF16),
                                   moe_w3[j].astype(BF16))
            y = expert_down(block_e, act, moe_w2[j].astype(BF16))
            xf = combine_residual(xf, jnp.take(y, dest[:, 0], axis=0), jnp.take(y, dest[:, 1], axis=0), route,
                                  gate_f, seq)
    return final_rms_norm(xf, final_norm).reshape(bsz, seq, d)
```

```python
import functools
import math

import jax
import jax.numpy as jnp
import numpy as np
from jax import lax
from jax.experimental import pallas as pl
from jax.experimental.pallas import tpu as pltpu

F32 = jnp.float32
BF16 = jnp.bfloat16
HIGHEST = lax.Precision.HIGHEST

LANES = 128
SUBLANES = 8
VMEM_LIMIT_BYTES = 48 * 1024 * 1024

NORM_EPS = 1e-6
ROPE_BASE = 10000.0
MLA_HEADS = 8
MLA_NOPE = 128
MLA_ROPE = 64
MLA_V = 128
MLA_QK_PAD = 256
GDN_HEADS = 8
GDN_DIM = 128
GDN_CHUNK = 64
LRU_BLOCKS = 8
LRU_C = 8.0
N_EXPERTS = 8
TOP_K = 2
MOE_BLOCK = 512
IN_FIXED_WIDTHS = (512, 256, MLA_ROPE, 3 * GDN_HEADS * GDN_DIM, GDN_HEADS * GDN_DIM, 4 * GDN_HEADS)
IN_FIXED_WIDTH = sum(IN_FIXED_WIDTHS)

P_GATE, P_QKV, P_Z, P_LRUX, P_LRUY, P_CQ, P_CKV, P_KROPE, P_AB, P_TOTAL = 0, 48, 72, 80, 88, 96, 100, 102, 103, 104


def _cparams(*semantics):
    return pltpu.CompilerParams(dimension_semantics=semantics, vmem_limit_bytes=VMEM_LIMIT_BYTES)


def _silu(x):
    return x * jax.nn.sigmoid(x)


def _softplus(x):
    return jnp.maximum(x, 0.0) + jnp.log1p(jnp.exp(-jnp.abs(x)))


def _bdot(a, b):
    return jnp.dot(a.astype(BF16), b.astype(BF16), preferred_element_type=F32)


def _bdot_nt(a, b):
    return lax.dot_general(a.astype(BF16), b.astype(BF16), (((1,), (1,)), ((), ())),
                           preferred_element_type=F32)


def _bdot_tn(a, b):
    return lax.dot_general(a.astype(BF16), b.astype(BF16), (((0,), (0,)), ((), ())),
                           preferred_element_type=F32)


def _fdot(a, b):
    return jnp.dot(a, b, preferred_element_type=F32, precision=HIGHEST)


def _ada_kernel(c_ref, w_ref, b_ref, o_ref):
    cond = _silu(c_ref[...])
    o_ref[...] = _bdot(cond, w_ref[...]) + b_ref[...]


def ada_modulation(c, ada_w, ada_b, *, tn=1024):
    n_layers, d, n = ada_w.shape
    bsz = c.shape[0]
    return pl.pallas_call(
        _ada_kernel,
        grid=(n_layers, n // tn),
        in_specs=[
            pl.BlockSpec((bsz, d), lambda l, j: (0, 0)),
            pl.BlockSpec((None, d, tn), lambda l, j: (l, 0, j)),
            pl.BlockSpec((None, 1, tn), lambda l, j: (l, 0, j)),
        ],
        out_specs=pl.BlockSpec((None, bsz, tn), lambda l, j: (l, 0, j)),
        out_shape=jax.ShapeDtypeStruct((n_layers, bsz, n), F32),
        compiler_params=_cparams("arbitrary", "arbitrary"),
        name="ada_modulation",
    )(c, ada_w, ada_b.reshape(n_layers, 1, n))


def _norm_mod(x, gain, scale, shift):
    y = x * lax.rsqrt(jnp.mean(x * x, axis=-1, keepdims=True) + NORM_EPS)
    return (y * gain) * (1.0 + scale) + shift


def _proj_kernel(x_ref, gain_ref, scale_ref, shift_ref, w_ref, o_ref, h_ref):
    @pl.when(pl.program_id(1) == 0)
    def _():
        h_ref[...] = _norm_mod(x_ref[...], gain_ref[...], scale_ref[...], shift_ref[...]).astype(BF16)

    o_ref[...] = jnp.dot(h_ref[...], w_ref[...], preferred_element_type=F32)


def _row_specs(d, tm, tiles_per_batch):
    return [
        pl.BlockSpec((tm, d), lambda i, j: (i, 0)),
        pl.BlockSpec((1, d), lambda i, j: (0, 0)),
        pl.BlockSpec((None, 1, d), lambda i, j: (i // tiles_per_batch, 0, 0)),
        pl.BlockSpec((None, 1, d), lambda i, j: (i // tiles_per_batch, 0, 0)),
    ]


def norm_projection(x, gain, scale, shift, w, seq, *, tm=512, tn=1024):
    t, d = x.shape
    n = w.shape[1]
    return pl.pallas_call(
        _proj_kernel,
        grid=(t // tm, n // tn),
        in_specs=_row_specs(d, tm, seq // tm) + [pl.BlockSpec((d, tn), lambda i, j: (0, j))],
        out_specs=pl.BlockSpec((tm, tn), lambda i, j: (i, j)),
        out_shape=jax.ShapeDtypeStruct((t, n), F32),
        scratch_shapes=[pltpu.VMEM((tm, d), BF16)],
        compiler_params=_cparams("arbitrary", "arbitrary"),
        name="norm_projection",
    )(x, gain.reshape(1, d), scale, shift, w)


def _rope_kernel(pos_ref, freq_ref, cos_ref, sin_ref):
    ang = pos_ref[...] * freq_ref[...]
    live = lax.broadcasted_iota(jnp.int32, ang.shape, 1) < MLA_ROPE
    cos_ref[...] = jnp.where(live, jnp.cos(ang), 0.0)
    sin_ref[...] = jnp.where(live, jnp.sin(ang), 0.0)


def rope_tables(positions, *, tm=1024):
    t = positions.size
    inv_freq = ROPE_BASE ** (-np.arange(0, MLA_ROPE, 2, dtype=np.float32) / MLA_ROPE)
    freq = np.zeros((1, LANES), np.float32)
    freq[0, :MLA_ROPE] = np.tile(inv_freq, 2)
    pos = positions.astype(F32).reshape(t, 1)
    return pl.pallas_call(
        _rope_kernel,
        grid=(t // tm,),
        in_specs=[pl.BlockSpec((tm, 1), lambda i: (i, 0)), pl.BlockSpec((1, LANES), lambda i: (0, 0))],
        out_specs=[pl.BlockSpec((tm, LANES), lambda i: (i, 0))] * 2,
        out_shape=[jax.ShapeDtypeStruct((t, LANES), F32)] * 2,
        compiler_params=_cparams("arbitrary"),
        name="rope_tables",
    )(pos, jnp.asarray(freq))


def _rope_half(t, cos, sin):
    lane = lax.broadcasted_iota(jnp.int32, t.shape, 1)
    rot = jnp.where(lane < MLA_ROPE // 2, -pltpu.roll(t, LANES - MLA_ROPE // 2, 1), pltpu.roll(t, MLA_ROPE // 2, 1))
    return t * cos + rot * sin


def _rms(x, gain):
    return x * lax.rsqrt(jnp.mean(x * x, axis=-1, keepdims=True) + NORM_EPS) * gain


def _mla_q_kernel(cq_ref, gain_ref, w_ref, cos_ref, sin_ref, o_ref):
    scale = (MLA_NOPE + MLA_ROPE) ** -0.5 * math.log2(math.e)
    q = _bdot(_rms(cq_ref[...], gain_ref[...]), w_ref[...])
    cos, sin = cos_ref[...], sin_ref[...]
    for h in range(MLA_HEADS):
        lo = h * MLA_QK_PAD
        o_ref[:, lo:lo + LANES] = (q[:, lo:lo + LANES] * scale).astype(BF16)
        o_ref[:, lo + LANES:lo + 2 * LANES] = (_rope_half(q[:, lo + LANES:lo + 2 * LANES], cos, sin) * scale).astype(BF16)


def mla_q(p, gain, w_q, cos, sin, *, tm=512):
    t = p.shape[0]
    lora = gain.shape[0]
    n = w_q.shape[1]
    return pl.pallas_call(
        _mla_q_kernel,
        grid=(t // tm,),
        in_specs=[
            pl.BlockSpec((tm, lora), lambda i: (i, P_CQ * LANES // lora)),
            pl.BlockSpec((1, lora), lambda i: (0, 0)),
            pl.BlockSpec((lora, n), lambda i: (0, 0)),
            pl.BlockSpec((tm, LANES), lambda i: (i, 0)),
            pl.BlockSpec((tm, LANES), lambda i: (i, 0)),
        ],
        out_specs=pl.BlockSpec((tm, n), lambda i: (i, 0)),
        out_shape=jax.ShapeDtypeStruct((t, n), BF16),
        compiler_params=_cparams("arbitrary"),
        name="mla_q",
    )(p, gain.reshape(1, lora), w_q, cos, sin)


def _mla_kv_kernel(ckv_ref, kr_ref, gain_ref, wk_ref, wv_ref, cos_ref, sin_ref, k_ref, v_ref):
    n = _rms(ckv_ref[...], gain_ref[...]).astype(BF16)
    k_nope = jnp.dot(n, wk_ref[...], preferred_element_type=F32)
    v = jnp.dot(n, wv_ref[...], preferred_element_type=F32)
    k_rope = _rope_half(kr_ref[...], cos_ref[...], sin_ref[...]).astype(BF16)
    ones_lane = (lax.broadcasted_iota(jnp.int32, k_rope.shape, 1) == 0).astype(BF16)
    for h in range(MLA_HEADS):
        lo = h * MLA_QK_PAD
        k_ref[:, lo:lo + LANES] = k_nope[:, h * MLA_NOPE:(h + 1) * MLA_NOPE].astype(BF16)
        k_ref[:, lo + LANES:lo + 2 * LANES] = k_rope
        v_ref[:, lo:lo + LANES] = v[:, h * MLA_V:(h + 1) * MLA_V].astype(BF16)
        v_ref[:, lo + LANES:lo + 2 * LANES] = ones_lane


def mla_kv(p, gain, w_k, w_v, cos, sin, *, tm=512):
    t = p.shape[0]
    lora = gain.shape[0]
    return pl.pallas_call(
        _mla_kv_kernel,
        grid=(t // tm,),
        in_specs=[
            pl.BlockSpec((tm, lora), lambda i: (i, P_CKV * LANES // lora)),
            pl.BlockSpec((tm, LANES), lambda i: (i, P_KROPE)),
            pl.BlockSpec((1, lora), lambda i: (0, 0)),
            pl.BlockSpec(w_k.shape, lambda i: (0, 0)),
            pl.BlockSpec(w_v.shape, lambda i: (0, 0)),
            pl.BlockSpec((tm, LANES), lambda i: (i, 0)),
            pl.BlockSpec((tm, LANES), lambda i: (i, 0)),
        ],
        out_specs=[
            pl.BlockSpec((tm, MLA_HEADS * MLA_QK_PAD), lambda i: (i, 0)),
            pl.BlockSpec((tm, MLA_HEADS * MLA_QK_PAD), lambda i: (i, 0)),
        ],
        out_shape=[
            jax.ShapeDtypeStruct((t, MLA_HEADS * MLA_QK_PAD), BF16),
            jax.ShapeDtypeStruct((t, MLA_HEADS * MLA_QK_PAD), BF16),
        ],
        compiler_params=_cparams("arbitrary"),
        name="mla_kv",
    )(p, p, gain.reshape(1, lora), w_k, w_v, cos, sin)


def _flash_kernel(q_ref, k_ref, v_ref, o_ref, m_ref, acc_ref, *, tk):
    m_ref[...] = jnp.full(m_ref.shape, -jnp.inf, F32)
    acc_ref[...] = jnp.zeros(acc_ref.shape, F32)
    q = q_ref[...]

    def body(i, _):
        rows = pl.ds(pl.multiple_of(i * tk, tk), tk)
        s = lax.dot_general(q, k_ref[rows, :], (((1,), (1,)), ((), ())), preferred_element_type=F32)
        m_prev = m_ref[...]
        m_new = jnp.maximum(m_prev, jnp.max(s, axis=-1, keepdims=True))
        p = jnp.exp2(s - m_new).astype(BF16)
        acc_ref[...] = jnp.exp2(m_prev - m_new) * acc_ref[...] + jnp.dot(p, v_ref[rows, :],
                                                                         preferred_element_type=F32)
        m_ref[...] = m_new
        return 0

    lax.fori_loop(0, k_ref.shape[0] // tk, body, 0, unroll=True)
    acc = acc_ref[...]
    o_ref[...] = (acc[:, :MLA_V] / acc[:, MLA_V:MLA_V + 1]).astype(o_ref.dtype)


def flash_attention(q, k, v, bsz, seq, *, tq=512, tk=1024):
    t = q.shape[0]
    tq, tk = min(tq, seq), min(tk, seq)
    nq = seq // tq
    return pl.pallas_call(
        functools.partial(_flash_kernel, tk=tk),
        grid=(bsz, MLA_HEADS, nq),
        in_specs=[
            pl.BlockSpec((tq, MLA_QK_PAD), lambda b, h, qi: (b * nq + qi, h)),
            pl.BlockSpec((seq, MLA_QK_PAD), lambda b, h, qi: (b, h)),
            pl.BlockSpec((seq, MLA_QK_PAD), lambda b, h, qi: (b, h)),
        ],
        out_specs=pl.BlockSpec((tq, MLA_V), lambda b, h, qi: (b * nq + qi, h)),
        out_shape=jax.ShapeDtypeStruct((t, MLA_HEADS * MLA_V), BF16),
        scratch_shapes=[pltpu.VMEM((tq, 1), F32), pltpu.VMEM((tq, MLA_QK_PAD), F32)],
        compiler_params=_cparams("arbitrary", "arbitrary", "arbitrary"),
        name="flash_attention",
    )(q, k, v)


def _conv4(main_ref, prev_ref, next_ref, w_ref, pad_ref, first, last):
    tm = main_ref.shape[0]
    x = main_ref[...]
    pad_ref[0:SUBLANES, :] = jnp.where(first, 0.0, prev_ref[...])
    pad_ref[SUBLANES:SUBLANES + tm, :] = x
    pad_ref[SUBLANES + tm:2 * SUBLANES + tm, :] = jnp.where(last, 0.0, next_ref[...])
    w = w_ref[...]
    return (w[0:1] * pad_ref[SUBLANES - 1:SUBLANES - 1 + tm, :] + w[1:2] * x
            + w[2:3] * pad_ref[SUBLANES + 1:SUBLANES + 1 + tm, :]
            + w[3:4] * pad_ref[SUBLANES + 2:SUBLANES + 2 + tm, :])


def _halo_specs(tm, width, col, n_row8, grid_rank):
    r8 = tm // SUBLANES
    if grid_rank == 1:
        return [
            pl.BlockSpec((tm, width), lambda i: (i, col)),
            pl.BlockSpec((SUBLANES, width), lambda i: (jnp.maximum(i * r8 - 1, 0), col)),
            pl.BlockSpec((SUBLANES, width), lambda i: (jnp.minimum((i + 1) * r8, n_row8 - 1), col)),
        ]
    return [
        pl.BlockSpec((tm, width), lambda i, j: (i, col + j)),
        pl.BlockSpec((SUBLANES, width), lambda i, j: (jnp.maximum(i * r8 - 1, 0), col + j)),
        pl.BlockSpec((SUBLANES, width), lambda i, j: (jnp.minimum((i + 1) * r8, n_row8 - 1), col + j)),
    ]


def _gdn_prep_kernel(main_ref, prev_ref, next_ref, w_ref, o_ref, pad_ref, *, tiles_per_batch):
    i, j = pl.program_id(0), pl.program_id(1)
    first = (i % tiles_per_batch) == 0
    last = (i % tiles_per_batch) == tiles_per_batch - 1
    y = _silu(_conv4(main_ref, prev_ref, next_ref, w_ref, pad_ref, first, last))

    @pl.when(j < 2)
    def _():
        post = jnp.where(j == 0, GDN_DIM ** -0.5, 1.0)
        for h in range(GDN_HEADS):
            seg = y[:, h * GDN_DIM:(h + 1) * GDN_DIM]
            inv = lax.rsqrt(jnp.sum(seg * seg, axis=-1, keepdims=True) + NORM_EPS)
            o_ref[:, h * GDN_DIM:(h + 1) * GDN_DIM] = seg * (inv * post)

    @pl.when(j == 2)
    def _():
        o_ref[...] = y


def gdn_prep(p, conv_w, seq, *, tm=512):
    t = p.shape[0]
    width = GDN_HEADS * GDN_DIM
    col = P_QKV * LANES // width
    return pl.pallas_call(
        functools.partial(_gdn_prep_kernel, tiles_per_batch=seq // tm),
        grid=(t // tm, 3),
        in_specs=_halo_specs(tm, width, col, t // SUBLANES, 2) + [pl.BlockSpec((4, width), lambda i, j: (0, j))],
        out_specs=pl.BlockSpec((tm, width), lambda i, j: (i, j)),
        out_shape=jax.ShapeDtypeStruct((t, 3 * width), F32),
        scratch_shapes=[pltpu.VMEM((tm + 2 * SUBLANES, width), F32)],
        compiler_params=_cparams("arbitrary", "arbitrary"),
        name="gdn_prep",
    )(p, p, p, conv_w)


def _unit_triangular_inverse(low, dot):
    c = low[0].shape[0]
    heads = range(len(low))
    row = lax.broadcasted_iota(jnp.int32, (c, c), 0)
    col = lax.broadcasted_iota(jnp.int32, (c, c), 1)
    eye = (row == col).astype(F32)
    inv = [eye - jnp.where((row ^ 1) == col, low[h], 0.0) for h in heads]
    shift = 1
    while (2 << shift) <= c:
        off_block = ((row >> shift) ^ 1) == (col >> shift)
        blk = [jnp.where(off_block, low[h], 0.0) for h in heads]
        half = [dot(inv[h], blk[h]) for h in heads]
        inv = [inv[h] - dot(half[h], inv[h]) for h in heads]
        shift += 1
    return inv


def _gdn_chunk_kernel(q_ref, k_ref, v_ref, ab_ref, alog_ref, dt_ref, o_ref, state_ref, *, reverse):
    c = q_ref.shape[0]
    heads = range(GDN_HEADS)

    @pl.when(pl.program_id(1) == 0)
    def _():
        state_ref[...] = jnp.zeros(state_ref.shape, F32)

    row = lax.broadcasted_iota(jnp.int32, (c, c), 0)
    col = lax.broadcasted_iota(jnp.int32, (c, c), 1)
    if reverse:
        incl, strict, cum = row <= col, row < col, (col >= row).astype(F32)
    else:
        incl, strict, cum = row >= col, row > col, (col <= row).astype(F32)
    eye_l = (lax.broadcasted_iota(jnp.int32, (LANES, LANES), 0)
             == lax.broadcasted_iota(jnp.int32, (LANES, LANES), 1)).astype(F32)

    ab = ab_ref[...]
    log_alpha = -jnp.exp(alog_ref[...]) * _softplus(ab + dt_ref[...])
    beta_all = jax.nn.sigmoid(ab)
    g_all = _fdot(cum, log_alpha)
    g_rows = lax.dot_general(eye_l, g_all, (((1,), (1,)), ((), ())), preferred_element_type=F32,
                             precision=HIGHEST)
    a_lane = 2 * GDN_HEADS if reverse else 0
    b_lane = a_lane + GDN_HEADS
    last = 0 if reverse else c - 1

    sl = [slice(h * GDN_DIM, (h + 1) * GDN_DIM) for h in heads]
    q = [q_ref[:, sl[h]] for h in heads]
    k = [k_ref[:, sl[h]] for h in heads]
    v = [v_ref[:, sl[h]] for h in heads]
    state = [state_ref[h] for h in heads]
    g_col = [g_all[:, a_lane + h:a_lane + h + 1] for h in heads]
    beta = [beta_all[:, b_lane + h:b_lane + h + 1] for h in heads]
    decay = [jnp.exp(jnp.where(incl, g_col[h] - g_rows[a_lane + h:a_lane + h + 1, :], -jnp.inf)) for h in heads]
    eg = [jnp.exp(g_col[h]) for h in heads]
    kb = [k[h] * beta[h] for h in heads]
    low = [jnp.where(strict, _bdot_nt(kb[h], k[h]) * decay[h], 0.0) for h in heads]
    qk = [_bdot_nt(q[h], k[h]) * decay[h] for h in heads]
    inv = _unit_triangular_inverse(low, _bdot)
    sol = [_bdot(inv[h], jnp.concatenate([v[h] * beta[h], kb[h] * eg[h]], axis=1)) for h in heads]
    v_new = [sol[h][:, :GDN_DIM] - _bdot(sol[h][:, GDN_DIM:], state[h]) for h in heads]
    out = [_bdot(q[h] * eg[h], state[h]) + _bdot(qk[h], v_new[h]) for h in heads]
    g_last = [g_col[h][last:last + 1, :] for h in heads]
    new_state = [state[h] * jnp.exp(g_last[h]) + _bdot_tn(k[h] * jnp.exp(g_last[h] - g_col[h]), v_new[h])
                 for h in heads]
    for h in heads:
        o_ref[:, sl[h]] = out[h]
        state_ref[h] = new_state[h]


def gdn_chunk_scan(qkv, p, a_log_row, dt_row, bsz, seq, *, reverse):
    t = qkv.shape[0]
    c = GDN_CHUNK
    n = seq // c
    width = GDN_HEADS * GDN_DIM

    def rows(b, i):
        return b * n + (n - 1 - i if reverse else i)

    return pl.pallas_call(
        functools.partial(_gdn_chunk_kernel, reverse=reverse),
        grid=(bsz, n),
        in_specs=[
            pl.BlockSpec((c, width), lambda b, i: (rows(b, i), 0)),
            pl.BlockSpec((c, width), lambda b, i: (rows(b, i), 1)),
            pl.BlockSpec((c, width), lambda b, i: (rows(b, i), 2)),
            pl.BlockSpec((c, LANES), lambda b, i: (rows(b, i), P_AB)),
            pl.BlockSpec((1, LANES), lambda b, i: (0, 0)),
            pl.BlockSpec((1, LANES), lambda b, i: (0, 0)),
        ],
        out_specs=pl.BlockSpec((c, width), lambda b, i: (rows(b, i), 0)),
        out_shape=jax.ShapeDtypeStruct((t, width), F32),
        scratch_shapes=[pltpu.VMEM((GDN_HEADS, GDN_DIM, GDN_DIM), F32)],
        compiler_params=_cparams("arbitrary", "arbitrary"),
        name="gdn_chunk_rev" if reverse else "gdn_chunk_fwd",
    )(qkv, qkv, qkv, p, a_log_row, dt_row)


def _gdn_out_kernel(of_ref, ob_ref, z_ref, gain_ref, o_ref):
    o = of_ref[...] + ob_ref[...]
    z = z_ref[...]
    gain = gain_ref[...]
    for h in range(GDN_HEADS):
        sl = slice(h * GDN_DIM, (h + 1) * GDN_DIM)
        o_ref[:, sl] = (_rms(o[:, sl], gain) * _silu(z[:, sl])).astype(BF16)


def gdn_output(o_f, o_b, p, out_norm, *, tm=512):
    t, width = o_f.shape
    return pl.pallas_call(
        _gdn_out_kernel,
        grid=(t // tm,),
        in_specs=[
            pl.BlockSpec((tm, width), lambda i: (i, 0)),
            pl.BlockSpec((tm, width), lambda i: (i, 0)),
            pl.BlockSpec((tm, width), lambda i: (i, P_Z * LANES // width)),
            pl.BlockSpec((1, GDN_DIM), lambda i: (0, 0)),
        ],
        out_specs=pl.BlockSpec((tm, width), lambda i: (i, 0)),
        out_shape=jax.ShapeDtypeStruct((t, width), BF16),
        compiler_params=_cparams("arbitrary"),
        name="gdn_output",
    )(o_f, o_b, p, out_norm.reshape(1, GDN_DIM))


def _lru_scan_tile(a, u, carry_ref, a_ref, b_ref, h_ref, reverse):
    tm = a.shape[0]
    sub = lax.broadcasted_iota(jnp.int32, a.shape, 0) % SUBLANES
    shift = 1
    while shift < SUBLANES:
        if reverse:
            ok = sub < SUBLANES - shift
            a_sh, u_sh = pltpu.roll(a, tm - shift, 0), pltpu.roll(u, tm - shift, 0)
        else:
            ok = sub >= shift
            a_sh, u_sh = pltpu.roll(a, shift, 0), pltpu.roll(u, shift, 0)
        u = a * jnp.where(ok, u_sh, 0.0) + u
        a = a * jnp.where(ok, a_sh, 1.0)
        shift *= 2
    a_ref[...] = a
    b_ref[...] = u
    groups = tm // SUBLANES
    edge = 0 if reverse else SUBLANES - 1

    def body(g, h_in):
        r0 = pl.multiple_of((groups - 1 - g if reverse else g) * SUBLANES, SUBLANES)
        h = a_ref[pl.ds(r0, SUBLANES), :] * h_in + b_ref[pl.ds(r0, SUBLANES), :]
        h_ref[pl.ds(r0, SUBLANES), :] = h
        return jnp.broadcast_to(h[edge:edge + 1, :], h.shape)

    carry_ref[...] = lax.fori_loop(0, groups, body, carry_ref[...], unroll=4)


def _lru_gates(xc, w_ref, ba_ref, bi_ref, lam_ref):
    blk = xc.shape[1] // LRU_BLOCKS
    parts = [_bdot(xc[:, n * blk:(n + 1) * blk], w_ref[n]) for n in range(LRU_BLOCKS)]
    r = jax.nn.sigmoid(jnp.concatenate([p[:, :blk] for p in parts], axis=1) + ba_ref[...])
    gate_i = jax.nn.sigmoid(jnp.concatenate([p[:, blk:] for p in parts], axis=1) + bi_ref[...])
    log_a = -LRU_C * r * _softplus(-lam_ref[...])
    a = jnp.exp(log_a)
    return a, xc * gate_i * jnp.sqrt(1.0 - a * a)


def _lru_fwd_kernel(main_ref, prev_ref, next_ref, cw_ref, cb_ref, w_ref, ba_ref, bi_ref, lam_ref,
                    h_ref, pad_ref, carry_ref, a_ref, b_ref, *, tiles_per_batch):
    i = pl.program_id(0) % tiles_per_batch

    @pl.when(i == 0)
    def _():
        carry_ref[...] = jnp.zeros(carry_ref.shape, F32)

    xc = _conv4(main_ref, prev_ref, next_ref, cw_ref, pad_ref, i == 0, i == tiles_per_batch - 1) + cb_ref[...]
    a, u = _lru_gates(xc, w_ref, ba_ref, bi_ref, lam_ref)
    _lru_scan_tile(a, u, carry_ref, a_ref, b_ref, h_ref, False)


def _lru_bwd_kernel(main_ref, prev_ref, next_ref, cw_ref, cb_ref, w_ref, ba_ref, bi_ref, lam_ref, hf_ref, y_ref,
                    o_ref, pad_ref, carry_ref, a_ref, b_ref, h_ref, *, tiles_per_batch):
    i = pl.program_id(0) % tiles_per_batch

    @pl.when(i == 0)
    def _():
        carry_ref[...] = jnp.zeros(carry_ref.shape, F32)

    xc = _conv4(main_ref, prev_ref, next_ref, cw_ref, pad_ref, i == tiles_per_batch - 1, i == 0) + cb_ref[...]
    a, u = _lru_gates(xc, w_ref, ba_ref, bi_ref, lam_ref)
    _lru_scan_tile(a, u, carry_ref, a_ref, b_ref, h_ref, True)
    o_ref[...] = ((hf_ref[...] + h_ref[...]) * jax.nn.gelu(y_ref[...], approximate=True)).astype(BF16)


def _lru_param_specs(width, blk):
    return [
        pl.BlockSpec((4, width), lambda i: (0, 0)),
        pl.BlockSpec((1, width), lambda i: (0, 0)),
        pl.BlockSpec((LRU_BLOCKS, blk, 2 * blk), lambda i: (0, 0, 0)),
        pl.BlockSpec((1, width), lambda i: (0, 0)),
        pl.BlockSpec((1, width), lambda i: (0, 0)),
        pl.BlockSpec((1, width), lambda i: (0, 0)),
    ]


def _lru_scratch(tm, width):
    return [pltpu.VMEM((tm + 2 * SUBLANES, width), F32), pltpu.VMEM((SUBLANES, width), F32),
            pltpu.VMEM((tm, width), F32), pltpu.VMEM((tm, width), F32)]


def rg_lru(p, conv_w, conv_b, w_a, b_a, w_i, b_i, lam, seq, *, tm=256):
    t = p.shape[0]
    width = conv_w.shape[1]
    blk = width // LRU_BLOCKS
    n_tiles = t // tm
    col_x, col_y = P_LRUX * LANES // width, P_LRUY * LANES // width
    w_dir = [jnp.concatenate([w_a[d], w_i[d]], axis=-1).astype(BF16) for d in range(2)]
    params = lambda d: (conv_w, conv_b.reshape(1, width), w_dir[d], b_a[d].reshape(1, width),
                        b_i[d].reshape(1, width), lam[d].reshape(1, width))
    h_f = pl.pallas_call(
        functools.partial(_lru_fwd_kernel, tiles_per_batch=seq // tm),
        grid=(n_tiles,),
        in_specs=_halo_specs(tm, width, col_x, t // SUBLANES, 1) + _lru_param_specs(width, blk),
        out_specs=pl.BlockSpec((tm, width), lambda i: (i, 0)),
        out_shape=jax.ShapeDtypeStruct((t, width), F32),
        scratch_shapes=_lru_scratch(tm, width),
        compiler_params=_cparams("arbitrary"),
        name="rg_lru_fwd",
    )(p, p, p, *params(0))
    r8 = tm // SUBLANES
    n_row8 = t // SUBLANES
    rev = lambda i: n_tiles - 1 - i
    halo_rev = [
        pl.BlockSpec((tm, width), lambda i: (rev(i), col_x)),
        pl.BlockSpec((SUBLANES, width), lambda i: (jnp.maximum(rev(i) * r8 - 1, 0), col_x)),
        pl.BlockSpec((SUBLANES, width), lambda i: (jnp.minimum((rev(i) + 1) * r8, n_row8 - 1), col_x)),
    ]
    return pl.pallas_call(
        functools.partial(_lru_bwd_kernel, tiles_per_batch=seq // tm),
        grid=(n_tiles,),
        in_specs=halo_rev + _lru_param_specs(width, blk) + [
            pl.BlockSpec((tm, width), lambda i: (rev(i), 0)),
            pl.BlockSpec((tm, width), lambda i: (rev(i), col_y)),
        ],
        out_specs=pl.BlockSpec((tm, width), lambda i: (rev(i), 0)),
        out_shape=jax.ShapeDtypeStruct((t, width), BF16),
        scratch_shapes=_lru_scratch(tm, width) + [pltpu.VMEM((tm, width), F32)],
        compiler_params=_cparams("arbitrary"),
        name="rg_lru_bwd",
    )(p, p, p, *params(1), h_f, p)


def _merge_kernel(oa_ref, ob_ref, oc_ref, wa_ref, wb_ref, wc_ref, ga_ref, gb_ref, gc_ref, o_ref):
    acc = jax.nn.sigmoid(ga_ref[...]) * jnp.dot(oa_ref[...], wa_ref[...], preferred_element_type=F32)
    acc += jax.nn.sigmoid(gb_ref[...]) * jnp.dot(ob_ref[...], wb_ref[...], preferred_element_type=F32)
    acc += jax.nn.sigmoid(gc_ref[...]) * jnp.dot(oc_ref[...], wc_ref[...], preferred_element_type=F32)
    o_ref[...] = acc.astype(BF16)


def merge_branches(o_a, o_b, o_c, w_a, w_b, w_c, p, *, tm=512, tn=512):
    t = o_a.shape[0]
    d = w_a.shape[1]
    n_col = d // tn
    branch = lambda arr: pl.BlockSpec((tm, arr.shape[1]), lambda j, i: (i, 0))
    weight = lambda arr: pl.BlockSpec((arr.shape[0], tn), lambda j, i: (0, j))
    gate = lambda g: pl.BlockSpec((tm, tn), lambda j, i: (i, P_GATE * LANES // tn + g * n_col + j))
    return pl.pallas_call(
        _merge_kernel,
        grid=(n_col, t // tm),
        in_specs=[branch(o_a), branch(o_b), branch(o_c), weight(w_a), weight(w_b), weight(w_c),
                  gate(0), gate(1), gate(2)],
        out_specs=pl.BlockSpec((tm, tn), lambda j, i: (i, j)),
        out_shape=jax.ShapeDtypeStruct((t, d), BF16),
        compiler_params=_cparams("arbitrary", "arbitrary"),
        name="merge_branches",
    )(o_a, o_b, o_c, w_a, w_b, w_c, p, p, p)


def _residual_kernel(a_ref, w_ref, x_ref, g_ref, o_ref):
    o_ref[...] = x_ref[...] + g_ref[...] * jnp.dot(a_ref[...], w_ref[...], preferred_element_type=F32)


def matmul_gated_residual(a, w, x, gate, seq, *, tm=512, tn=512):
    t, k = a.shape
    d = w.shape[1]
    tiles_per_batch = seq // tm
    return pl.pallas_call(
        _residual_kernel,
        grid=(t // tm, d // tn),
        in_specs=[
            pl.BlockSpec((tm, k), lambda i, j: (i, 0)),
            pl.BlockSpec((k, tn), lambda i, j: (0, j)),
            pl.BlockSpec((tm, tn), lambda i, j: (i, j)),
            pl.BlockSpec((None, 1, tn), lambda i, j: (i // tiles_per_batch, 0, j)),
        ],
        out_specs=pl.BlockSpec((tm, tn), lambda i, j: (i, j)),
        out_shape=jax.ShapeDtypeStruct((t, d), F32),
        compiler_params=_cparams("arbitrary", "arbitrary"),
        name="matmul_gated_residual",
    )(a, w, x, gate)


def _swiglu_up_kernel(x_ref, gain_ref, scale_ref, shift_ref, w1_ref, w3_ref, o_ref, h_ref):
    @pl.when(pl.program_id(1) == 0)
    def _():
        h_ref[...] = _norm_mod(x_ref[...], gain_ref[...], scale_ref[...], shift_ref[...]).astype(BF16)

    h = h_ref[...]
    up = jnp.dot(h, w1_ref[...], preferred_element_type=F32)
    o_ref[...] = (_silu(up) * jnp.dot(h, w3_ref[...], preferred_element_type=F32)).astype(BF16)


def norm_swiglu_up(x, gain, scale, shift, w1, w3, seq, *, tm=512, tn=512):
    t, d = x.shape
    f = w1.shape[1]
    return pl.pallas_call(
        _swiglu_up_kernel,
        grid=(t // tm, f // tn),
        in_specs=_row_specs(d, tm, seq // tm) + [pl.BlockSpec((d, tn), lambda i, j: (0, j))] * 2,
        out_specs=pl.BlockSpec((tm, tn), lambda i, j: (i, j)),
        out_shape=jax.ShapeDtypeStruct((t, f), BF16),
        scratch_shapes=[pltpu.VMEM((tm, d), BF16)],
        compiler_params=_cparams("arbitrary", "arbitrary"),
        name="norm_swiglu_up",
    )(x, gain.reshape(1, d), scale, shift, w1, w3)


def _router_kernel(x_ref, gain_ref, scale_ref, shift_ref, w_ref, b_ref, h_ref, route_ref):
    h = _norm_mod(x_ref[...], gain_ref[...], scale_ref[...], shift_ref[...])
    h_ref[...] = h
    lane =lax.broadcasted_iota(jnp.int32, route_ref.shape, 1)
    logits = jnp.where(lane < N_EXPERTS, _fdot(h, w_ref[...]) + b_ref[...], -jnp.inf)
    top1 = jnp.max(logits, axis=-1, keepdims=True)
    idx1 = jnp.min(jnp.where(logits == top1, lane, LANES), axis=-1, keepdims=True)
    rest = jnp.where(lane == idx1, -jnp.inf, logits)
    top2 = jnp.max(rest, axis=-1, keepdims=True)
    idx2 = jnp.min(jnp.where(rest == top2, lane, LANES), axis=-1, keepdims=True)
    e2 = jnp.exp(top2 - top1)
    inv = 1.0 / (1.0 + e2)
    route_ref[...] = jnp.where(lane == 0, idx1.astype(F32),
                               jnp.where(lane == 1, idx2.astype(F32),
                                         jnp.where(lane == 2, inv, jnp.where(lane == 3, e2 * inv, 0.0))))


def norm_router(x, gain, scale, shift, w_router, b_router, seq, *, tm=512):
    t, d = x.shape
    w_pad = jnp.zeros((d, LANES), F32).at[:, :N_EXPERTS].set(w_router)
    b_pad = jnp.zeros((1, LANES), F32).at[0, :N_EXPERTS].set(b_router)
    tiles_per_batch = seq // tm
    return pl.pallas_call(
        _router_kernel,
        grid=(t // tm,),
        in_specs=[
            pl.BlockSpec((tm, d), lambda i: (i, 0)),
            pl.BlockSpec((1, d), lambda i: (0, 0)),
            pl.BlockSpec((None, 1, d), lambda i: (i // tiles_per_batch, 0, 0)),
            pl.BlockSpec((None, 1, d), lambda i: (i // tiles_per_batch, 0, 0)),
            pl.BlockSpec((d, LANES), lambda i: (0, 0)),
            pl.BlockSpec((1, LANES), lambda i: (0, 0)),
        ],
        out_specs=[pl.BlockSpec((tm, d), lambda i: (i, 0)), pl.BlockSpec((tm, LANES), lambda i: (i, 0))],
        out_shape=[jax.ShapeDtypeStruct((t, d), F32), jax.ShapeDtypeStruct((t, LANES), F32)],
        compiler_params=_cparams("arbitrary"),
        name="norm_router",
    )(x, gain.reshape(1, d), scale, shift, w_pad, b_pad)


def _gather_rows(src_hbm, jobs, sem_idx, sem_rows):
    for idx_ref, idx_smem, _ in jobs:
        load = pltpu.make_async_copy(idx_ref, idx_smem, sem_idx)
        load.start()
        load.wait()

    def row_copy(buf_ref, r, src_row):
        return pltpu.make_async_copy(src_hbm.at[pl.ds(src_row, 1)], buf_ref.at[pl.ds(r, 1)], sem_rows)

    for _, idx_smem, buf_ref in jobs:
        def issue(r, carry, idx_smem=idx_smem, buf_ref=buf_ref):
            row_copy(buf_ref, r, idx_smem[0, 0, r]).start()
            return carry

        lax.fori_loop(0, buf_ref.shape[0], issue, 0, unroll=8)
    for _, _, buf_ref in jobs:
        def drain(r, carry, buf_ref=buf_ref):
            row_copy(buf_ref, r, 0).wait()
            return carry

        lax.fori_loop(0, buf_ref.shape[0], drain, 0, unroll=8)


def _dispatch_kernel(used_ref, idx_ref, h_hbm, o_ref, buf_ref, idx_smem, sem_idx, sem_rows):
    live = pl.program_id(0) < used_ref[0]

    @pl.when(live)
    def _():
        _gather_rows(h_hbm, [(idx_ref, idx_smem, buf_ref)], sem_idx, sem_rows)
        o_ref[...] = buf_ref[...].astype(BF16)

    @pl.when(jnp.logical_not(live))
    def _():
        o_ref[...] = jnp.zeros(o_ref.shape, BF16)


def expert_dispatch(n_used, row_tok, h):
    d = h.shape[1]
    n_blocks = row_tok.shape[0] // MOE_BLOCK
    return pl.pallas_call(
        _dispatch_kernel,
        grid_spec=pltpu.PrefetchScalarGridSpec(
            num_scalar_prefetch=1,
            grid=(n_blocks,),
            in_specs=[
                pl.BlockSpec((1, 1, MOE_BLOCK), lambda i, used: (i, 0, 0)),
                pl.BlockSpec(memory_space=pl.ANY),
            ],
            out_specs=pl.BlockSpec((MOE_BLOCK, d), lambda i, used: (i, 0)),
            scratch_shapes=[pltpu.VMEM((MOE_BLOCK, d), F32), pltpu.SMEM((1, 1, MOE_BLOCK), jnp.int32),
                            pltpu.SemaphoreType.DMA(()), pltpu.SemaphoreType.DMA(())],
        ),
        out_shape=jax.ShapeDtypeStruct((n_blocks * MOE_BLOCK, d), BF16),
        compiler_params=_cparams("arbitrary"),
        name="expert_dispatch",
    )(n_used, row_tok.reshape(n_blocks, 1, MOE_BLOCK), h)


def _expert_up_kernel(be_ref, used_ref, x_ref, w1_ref, w3_ref, o_ref, w1b_ref, w3b_ref):
    i = pl.program_id(1)

    @pl.when(jnp.logical_or(i == 0, be_ref[i] != be_ref[jnp.maximum(i - 1, 0)]))
    def _():
        w1b_ref[...] = w1_ref[...].astype(BF16)
        w3b_ref[...] = w3_ref[...].astype(BF16)

    @pl.when(i < used_ref[0])
    def _():
        x = x_ref[...]
        up = jnp.dot(x, w1b_ref[...], preferred_element_type=F32)
        o_ref[...] = (_silu(up) * jnp.dot(x, w3b_ref[...], preferred_element_type=F32)).astype(BF16)

    @pl.when(i >= used_ref[0])
    def _():
        o_ref[...] = jnp.zeros(o_ref.shape, BF16)


def expert_swiglu_up(block_e, n_used, xs, w1, w3, *, tn=512):
    rows, d = xs.shape
    f = w1.shape[2]
    return pl.pallas_call(
        _expert_up_kernel,
        grid_spec=pltpu.PrefetchScalarGridSpec(
            num_scalar_prefetch=2,
            grid=(f // tn, rows // MOE_BLOCK),
            in_specs=[
                pl.BlockSpec((MOE_BLOCK, d), lambda j, i, be, used: (i, 0)),
                pl.BlockSpec((None, d, tn), lambda j, i, be, used: (be[i], 0, j)),
                pl.BlockSpec((None, d, tn), lambda j, i, be, used: (be[i], 0, j)),
            ],
            out_specs=pl.BlockSpec((MOE_BLOCK, tn), lambda j, i, be, used: (i, j)),
            scratch_shapes=[pltpu.VMEM((d, tn), BF16), pltpu.VMEM((d, tn), BF16)],
        ),
        out_shape=jax.ShapeDtypeStruct((rows, f), BF16),
        compiler_params=_cparams("arbitrary", "arbitrary"),
        name="expert_swiglu_up",
    )(block_e, n_used, xs, w1, w3)


def _expert_down_kernel(be_ref, used_ref, a_ref, w_ref, o_ref):
    i = pl.program_id(1)

    @pl.when(i < used_ref[0])
    def _():
        o_ref[...] = jnp.dot(a_ref[...], w_ref[...], preferred_element_type=F32)

    @pl.when(i >= used_ref[0])
    def _():
        o_ref[...] = jnp.zeros(o_ref.shape, F32)


def expert_down(block_e, n_used, act, w2, *, tn=512):
    rows, f = act.shape
    d = w2.shape[2]
    return pl.pallas_call(
        _expert_down_kernel,
        grid_spec=pltpu.PrefetchScalarGridSpec(
            num_scalar_prefetch=2,
            grid=(d // tn, rows // MOE_BLOCK),
            in_specs=[
                pl.BlockSpec((MOE_BLOCK, f), lambda j, i, be, used: (i, 0)),
                pl.BlockSpec((None, f, tn), lambda j, i, be, used: (be[i], 0, j)),
            ],
            out_specs=pl.BlockSpec((MOE_BLOCK, tn), lambda j, i, be, used: (i, j)),
        ),
        out_shape=jax.ShapeDtypeStruct((rows, d), F32),
        compiler_params=_cparams("arbitrary", "arbitrary"),
        name="expert_down",
    )(block_e, n_used, act, w2)


def _combine_kernel(idx1_ref, idx2_ref, x_ref, route_ref, g_ref, y_hbm, o_ref,
                    buf1_ref, buf2_ref, idx1_smem, idx2_smem, sem_idx, sem_rows):
    _gather_rows(y_hbm, [(idx1_ref, idx1_smem, buf1_ref), (idx2_ref, idx2_smem, buf2_ref)], sem_idx, sem_rows)
    route = route_ref[...]
    y = route[:, 2:3] * buf1_ref[...] + route[:, 3:4] * buf2_ref[...]
    o_ref[...] = x_ref[...] + g_ref[...] * y


def combine_residual(x, y, dest, route, gate, seq, *, tm=256):
    t, d = x.shape
    tiles_per_batch = seq // tm
    n_tiles = t // tm
    row = pl.BlockSpec((tm, d), lambda i: (i, 0))
    idx = pl.BlockSpec((1, 1, tm), lambda i: (i, 0, 0))
    return pl.pallas_call(
        _combine_kernel,
        grid=(n_tiles,),
        in_specs=[idx, idx, row, pl.BlockSpec((tm, LANES), lambda i: (i, 0)),
                  pl.BlockSpec((None, 1, d), lambda i: (i // tiles_per_batch, 0, 0)),
                  pl.BlockSpec(memory_space=pl.ANY)],
        out_specs=row,
        out_shape=jax.ShapeDtypeStruct((t, d), F32),
        scratch_shapes=[pltpu.VMEM((tm, d), F32), pltpu.VMEM((tm, d), F32),
                        pltpu.SMEM((1, 1, tm), jnp.int32), pltpu.SMEM((1, 1, tm), jnp.int32),
                        pltpu.SemaphoreType.DMA(()), pltpu.SemaphoreType.DMA(())],
        compiler_params=_cparams("arbitrary"),
        name="combine_residual",
    )(dest[:, 0].reshape(n_tiles, 1, tm), dest[:, 1].reshape(n_tiles, 1, tm), x, route, gate, y)


def routing_plan(route, n_tokens):
    n_pairs = n_tokens * TOP_K
    flat_e = route[:, :TOP_K].astype(jnp.int32).reshape(n_pairs)
    onehot = (flat_e[:, None] == jnp.arange(N_EXPERTS, dtype=jnp.int32)[None, :]).astype(jnp.int32)
    running = jnp.cumsum(onehot, axis=0)
    counts = running[-1]
    rank = jnp.take_along_axis(running, flat_e[:, None], axis=1)[:, 0] - 1
    padded = (counts + MOE_BLOCK - 1) // MOE_BLOCK * MOE_BLOCK
    pend = jnp.cumsum(padded)
    dest = (pend - padded)[flat_e] + rank
    n_blocks = -(-n_pairs // MOE_BLOCK) + N_EXPERTS
    flat_tok = jnp.repeat(jnp.arange(n_tokens, dtype=jnp.int32), TOP_K)
    row_tok = jnp.zeros((n_blocks * MOE_BLOCK,), jnp.int32).at[dest].set(flat_tok)
    block_e = jnp.minimum(jnp.searchsorted(pend, jnp.arange(n_blocks, dtype=jnp.int32) * MOE_BLOCK, side="right"),
                          N_EXPERTS - 1).astype(jnp.int32)
    n_used = (pend[-1:] // MOE_BLOCK).astype(jnp.int32)
    return row_tok, dest.reshape(n_tokens, TOP_K), block_e, n_used


def _final_norm_kernel(x_ref, gain_ref, o_ref):
    o_ref[...] = _rms(x_ref[...], gain_ref[...])


def final_rms_norm(x, gain, *, tm=512):
    t, d = x.shape
    return pl.pallas_call(
        _final_norm_kernel,
        grid=(t // tm,),
        in_specs=[pl.BlockSpec((tm, d), lambda i: (i, 0)), pl.BlockSpec((1, d), lambda i: (0, 0))],
        out_specs=pl.BlockSpec((tm, d), lambda i: (i, 0)),
        out_shape=jax.ShapeDtypeStruct((t, d), F32),
        compiler_params=_cparams("arbitrary"),
        name="final_norm",
    )(x, gain.reshape(1, d))


def _pad_cols(w, width):
    return jnp.pad(w, ((0, 0), (0, width - w.shape[1])))


def projection_weight(w_in, w_gate):
    lru_width = (w_in.shape[1] - IN_FIXED_WIDTH) // 2
    o = np.cumsum((0,) + IN_FIXED_WIDTHS + (lru_width, lru_width))
    c_q, c_kv, k_rope, qkv, z, ab, lru_x, lru_y = (w_in[:, int(o[k]):int(o[k + 1])] for k in range(8))
    parts = [w_gate, qkv, z, lru_x, lru_y, c_q, c_kv, _pad_cols(k_rope, LANES), _pad_cols(ab, LANES)]
    return jnp.concatenate(parts, axis=1).astype(BF16)


def mla_weights(w_uq, w_ukv):
    d_q, d_kv = w_uq.shape[0], w_ukv.shape[0]
    wq = w_uq.reshape(d_q, MLA_HEADS, MLA_NOPE + MLA_ROPE)
    wq = jnp.pad(wq, ((0, 0), (0, 0), (0, MLA_QK_PAD - MLA_NOPE - MLA_ROPE))).reshape(d_q, MLA_HEADS * MLA_QK_PAD)
    wkv = w_ukv.reshape(d_kv, MLA_HEADS, MLA_NOPE + MLA_V)
    wk = wkv[:, :, :MLA_NOPE].reshape(d_kv, MLA_HEADS * MLA_NOPE)
    wv = wkv[:, :, MLA_NOPE:].reshape(d_kv, MLA_HEADS * MLA_V)
    return wq.astype(BF16), wk.astype(BF16), wv.astype(BF16)


def _gdn_gate_rows(a_log, dt_bias):
    zeros = jnp.zeros((GDN_HEADS,), F32)
    row = lambda v: _pad_cols(jnp.concatenate([v[0], zeros, v[1], zeros])[None, :], LANES)
    return row(a_log.astype(F32)), row(dt_bias.astype(F32))


def kernel(x, c, positions, ada_w, ada_b, norm_mix, norm_ffn, w_in, mla_q_norm, mla_w_uq, mla_kv_norm,
           mla_w_ukv, gdn_conv, gdn_a_log, gdn_dt_bias, gdn_out_norm, lru_conv_w, lru_conv_b, lru_w_a,
           lru_b_a, lru_w_i, lru_b_i, lru_lambda, w_branch_a, w_branch_b, w_branch_c, w_gate, w_out,
           ffn_w1, ffn_w3, ffn_w2, moe_router, moe_router_b, moe_w1, moe_w3, moe_w2, final_norm):
    bsz, seq, d = x.shape
    t = bsz * seq
    depth = ada_w.shape[0]
    xf = x.reshape(t, d)
    mod = ada_modulation(c, ada_w, ada_b).reshape(depth, bsz, 6, 1, d)
    cos, sin = rope_tables(positions)
    for layer in range(depth):
        shift_m, scale_m, gate_m, shift_f, scale_f, gate_f = (mod[layer, :, k] for k in range(6))
        p = norm_projection(xf, norm_mix[layer], scale_m, shift_m, projection_weight(w_in[layer], w_gate[layer]), seq)
        wq, wk, wv = mla_weights(mla_w_uq[layer], mla_w_ukv[layer])
        q = mla_q(p, mla_q_norm[layer], wq, cos, sin)
        k, v = mla_kv(p, mla_kv_norm[layer], wk, wv, cos, sin)
        o_a = flash_attention(q, k, v, bsz, seq)
        qkv = gdn_prep(p, gdn_conv[layer], seq)
        a_log_row, dt_row = _gdn_gate_rows(gdn_a_log[layer], gdn_dt_bias[layer])
        o_f = gdn_chunk_scan(qkv, p, a_log_row, dt_row, bsz, seq, reverse=False)
        o_r = gdn_chunk_scan(qkv, p, a_log_row, dt_row, bsz, seq, reverse=True)
        o_b = gdn_output(o_f, o_r, p, gdn_out_norm[layer])
        o_c = rg_lru(p, lru_conv_w[layer], lru_conv_b[layer], lru_w_a[layer], lru_b_a[layer], lru_w_i[layer],
                     lru_b_i[layer], lru_lambda[layer], seq)
        merged = merge_branches(o_a, o_b, o_c, w_branch_a[layer].astype(BF16), w_branch_b[layer].astype(BF16),
                                w_branch_c[layer].astype(BF16), p)
        xf = matmul_gated_residual(merged, w_out[layer].astype(BF16), xf, gate_m, seq)
        j = layer // 2
        if layer % 2 == 0:
            act = norm_swiglu_up(xf, norm_ffn[layer], scale_f, shift_f, ffn_w1[j].astype(BF16),
                                 ffn_w3[j].astype(BF16), seq)
            xf = matmul_gated_residual(act, ffn_w2[j].astype(BF16), xf, gate_f, seq)
        else:
            h, route = norm_router(xf, norm_ffn[layer], scale_f, shift_f, moe_router[j], moe_router_b[j], seq)
            row_tok, dest, block_e, n_used = routing_plan(route, t)
            act = expert_swiglu_up(block_e, n_used, expert_dispatch(n_used, row_tok, h), moe_w1[j], moe_w3[j])
            y = expert_down(block_e, n_used, act, moe_w2[j].astype(BF16))
            xf = combine_residual(xf, y, dest, route, gate_f, seq)
    return final_rms_norm(xf, final_norm).reshape(bsz, seq, d)
```

```python
import functools
import math

import jax
import jax.numpy as jnp
import numpy as np
from jax import lax
from jax.experimental import pallas as pl
from jax.experimental.pallas import tpu as pltpu

F32 = jnp.float32
BF16 = jnp.bfloat16
HIGHEST = lax.Precision.HIGHEST

LANES = 128
SUBLANES = 8
VMEM_LIMIT_BYTES = 56 * 1024 * 1024

NORM_EPS = 1e-6
ROPE_BASE = 10000.0
MLA_HEADS = 8
MLA_NOPE = 128
MLA_ROPE = 64
MLA_V = 128
MLA_QK_PAD = 256
GDN_HEADS = 8
GDN_DIM = 128
GDN_CHUNK = 64
LRU_BLOCKS = 8
LRU_C = 8.0
N_EXPERTS = 8
TOP_K = 2
MOE_BLOCK = 512
IN_FIXED_WIDTHS = (512, 256, MLA_ROPE, 3 * GDN_HEADS * GDN_DIM, GDN_HEADS * GDN_DIM, 4 * GDN_HEADS)
IN_FIXED_WIDTH = sum(IN_FIXED_WIDTHS)

P_GATE, P_QKV, P_Z, P_LRUX, P_LRUY, P_CQ, P_CKV, P_KROPE, P_AB, P_TOTAL = 0, 48, 72, 80, 88, 96, 100, 102, 103, 104


def _cparams(*semantics):
    return pltpu.CompilerParams(dimension_semantics=semantics, vmem_limit_bytes=VMEM_LIMIT_BYTES)


def _silu(x):
    return x * jax.nn.sigmoid(x)


def _softplus(x):
    return jnp.maximum(x, 0.0) + jnp.log1p(jnp.exp(-jnp.abs(x)))


def _bdot(a, b):
    return jnp.dot(a.astype(BF16), b.astype(BF16), preferred_element_type=F32)


def _bdot_nt(a, b):
    return lax.dot_general(a.astype(BF16), b.astype(BF16), (((1,), (1,)), ((), ())),
                           preferred_element_type=F32)


def _bdot_tn(a, b):
    return lax.dot_general(a.astype(BF16), b.astype(BF16), (((0,), (0,)), ((), ())),
                           preferred_element_type=F32)


def _fdot(a, b):
    return jnp.dot(a, b, preferred_element_type=F32, precision=HIGHEST)


def _ada_kernel(c_ref, w_ref, b_ref, o_ref):
    cond = _silu(c_ref[...])
    o_ref[...] = _bdot(cond, w_ref[...]) + b_ref[...]


def ada_modulation(c, ada_w, ada_b, *, tn=1024):
    n_layers, d, n = ada_w.shape
    bsz = c.shape[0]
    return pl.pallas_call(
        _ada_kernel,
        grid=(n_layers, n // tn),
        in_specs=[
            pl.BlockSpec((bsz, d), lambda l, j: (0, 0)),
            pl.BlockSpec((None, d, tn), lambda l, j: (l, 0, j)),
            pl.BlockSpec((None, 1, tn), lambda l, j: (l, 0, j)),
        ],
        out_specs=pl.BlockSpec((None, bsz, tn), lambda l, j: (l, 0, j)),
        out_shape=jax.ShapeDtypeStruct((n_layers, bsz, n), F32),
        compiler_params=_cparams("arbitrary", "arbitrary"),
        name="ada_modulation",
    )(c, ada_w, ada_b.reshape(n_layers, 1, n))


def _norm_mod(x, gain, scale, shift):
    y = x * lax.rsqrt(jnp.mean(x * x, axis=-1, keepdims=True) + NORM_EPS)
    return (y * gain) * (1.0 + scale) + shift


NORM_SLAB_ROWS = 256


def _store_norm_mod(x_ref, gain_ref, scale_ref, shift_ref, h_ref):
    gain, scale, shift = gain_ref[...], scale_ref[...], shift_ref[...]
    for r in range(0, x_ref.shape[0], NORM_SLAB_ROWS):
        rows = slice(r, r + NORM_SLAB_ROWS)
        h_ref[rows, :] = _norm_mod(x_ref[rows, :], gain, scale, shift).astype(BF16)


def _proj_kernel(x_ref, gain_ref, scale_ref, shift_ref, w_ref, o_ref, h_ref):
    @pl.when(pl.program_id(1) == 0)
    def _():
        _store_norm_mod(x_ref, gain_ref, scale_ref, shift_ref, h_ref)

    o_ref[...] = jnp.dot(h_ref[...], w_ref[...], preferred_element_type=F32)


def _row_specs(d, tm, tiles_per_batch):
    return [
        pl.BlockSpec((tm, d), lambda i, j: (i, 0)),
        pl.BlockSpec((1, d), lambda i, j: (0, 0)),
        pl.BlockSpec((None, 1, d), lambda i, j: (i // tiles_per_batch, 0, 0)),
        pl.BlockSpec((None, 1, d), lambda i, j: (i // tiles_per_batch, 0, 0)),
    ]


def norm_projection(x, gain, scale, shift, w, seq, *, tm=1024, tn=512):
    t, d = x.shape
    n = w.shape[1]
    tm = min(tm, seq)
    return pl.pallas_call(
        _proj_kernel,
        grid=(t // tm, n // tn),
        in_specs=_row_specs(d, tm, seq // tm) + [pl.BlockSpec((d, tn), lambda i, j: (0, j))],
        out_specs=pl.BlockSpec((tm, tn), lambda i, j: (i, j)),
        out_shape=jax.ShapeDtypeStruct((t, n), F32),
        scratch_shapes=[pltpu.VMEM((tm, d), BF16)],
        compiler_params=_cparams("arbitrary", "arbitrary"),
        name="norm_projection",
    )(x, gain.reshape(1, d), scale, shift, w)


def _rope_kernel(pos_ref, freq_ref, cos_ref, sin_ref):
    ang = pos_ref[...] * freq_ref[...]
    live = lax.broadcasted_iota(jnp.int32, ang.shape, 1) < MLA_ROPE
    cos_ref[...] = jnp.where(live, jnp.cos(ang), 0.0)
    sin_ref[...] = jnp.where(live, jnp.sin(ang), 0.0)


def rope_tables(positions, *, tm=1024):
    t = positions.size
    inv_freq = ROPE_BASE ** (-np.arange(0, MLA_ROPE, 2, dtype=np.float32) / MLA_ROPE)
    freq = np.zeros((1, LANES), np.float32)
    freq[0, :MLA_ROPE] = np.tile(inv_freq, 2)
    pos = positions.astype(F32).reshape(t, 1)
    return pl.pallas_call(
        _rope_kernel,
        grid=(t // tm,),
        in_specs=[pl.BlockSpec((tm, 1), lambda i: (i, 0)), pl.BlockSpec((1, LANES), lambda i: (0, 0))],
        out_specs=[pl.BlockSpec((tm, LANES), lambda i: (i, 0))] * 2,
        out_shape=[jax.ShapeDtypeStruct((t, LANES), F32)] * 2,
        compiler_params=_cparams("arbitrary"),
        name="rope_tables",
    )(pos, jnp.asarray(freq))


def _rope_half(t, cos, sin):
    lane = lax.broadcasted_iota(jnp.int32, t.shape, 1)
    rot = jnp.where(lane < MLA_ROPE // 2, -pltpu.roll(t, LANES - MLA_ROPE // 2, 1), pltpu.roll(t, MLA_ROPE // 2, 1))
    return t * cos + rot * sin


def _rms(x, gain):
    return x * lax.rsqrt(jnp.mean(x * x, axis=-1, keepdims=True) + NORM_EPS) * gain


def _mla_q_kernel(cq_ref, gain_ref, w_ref, cos_ref, sin_ref, o_ref):
    scale = (MLA_NOPE + MLA_ROPE) ** -0.5 * math.log2(math.e)
    q = _bdot(_rms(cq_ref[...], gain_ref[...]), w_ref[...])
    cos, sin = cos_ref[...], sin_ref[...]
    for h in range(MLA_HEADS):
        lo = h * MLA_QK_PAD
        o_ref[:, lo:lo + LANES] = (q[:, lo:lo + LANES] * scale).astype(BF16)
        o_ref[:, lo + LANES:lo + 2 * LANES] = (_rope_half(q[:, lo + LANES:lo + 2 * LANES], cos, sin) * scale).astype(BF16)


def mla_q(p, gain, w_q, cos, sin, *, tm=512):
    t = p.shape[0]
    lora = gain.shape[0]
    n = w_q.shape[1]
    return pl.pallas_call(
        _mla_q_kernel,
        grid=(t // tm,),
        in_specs=[
            pl.BlockSpec((tm, lora), lambda i: (i, P_CQ * LANES // lora)),
            pl.BlockSpec((1, lora), lambda i: (0, 0)),
            pl.BlockSpec((lora, n), lambda i: (0, 0)),
            pl.BlockSpec((tm, LANES), lambda i: (i, 0)),
            pl.BlockSpec((tm, LANES), lambda i: (i, 0)),
        ],
        out_specs=pl.BlockSpec((tm, n), lambda i: (i, 0)),
        out_shape=jax.ShapeDtypeStruct((t, n), BF16),
        compiler_params=_cparams("arbitrary"),
        name="mla_q",
    )(p, gain.reshape(1, lora), w_q, cos, sin)


def _mla_kv_kernel(ckv_ref, kr_ref, gain_ref, wk_ref, wv_ref, cos_ref, sin_ref, k_ref, v_ref):
    n = _rms(ckv_ref[...], gain_ref[...]).astype(BF16)
    k_nope = jnp.dot(n, wk_ref[...], preferred_element_type=F32)
    v = jnp.dot(n, wv_ref[...], preferred_element_type=F32)
    k_rope = _rope_half(kr_ref[...], cos_ref[...], sin_ref[...]).astype(BF16)
    ones_lane = (lax.broadcasted_iota(jnp.int32, k_rope.shape, 1) == 0).astype(BF16)
    for h in range(MLA_HEADS):
        lo = h * MLA_QK_PAD
        k_ref[:, lo:lo + LANES] = k_nope[:, h * MLA_NOPE:(h + 1) * MLA_NOPE].astype(BF16)
        k_ref[:, lo + LANES:lo + 2 * LANES] = k_rope
        v_ref[:, lo:lo + LANES] = v[:, h * MLA_V:(h + 1) * MLA_V].astype(BF16)
        v_ref[:, lo + LANES:lo + 2 * LANES] = ones_lane


def mla_kv(p, gain, w_k, w_v, cos, sin, *, tm=512):
    t = p.shape[0]
    lora = gain.shape[0]
    return pl.pallas_call(
        _mla_kv_kernel,
        grid=(t // tm,),
        in_specs=[
            pl.BlockSpec((tm, lora), lambda i: (i, P_CKV * LANES // lora)),
            pl.BlockSpec((tm, LANES), lambda i: (i, P_KROPE)),
            pl.BlockSpec((1, lora), lambda i: (0, 0)),
            pl.BlockSpec(w_k.shape, lambda i: (0, 0)),
            pl.BlockSpec(w_v.shape, lambda i: (0, 0)),
            pl.BlockSpec((tm, LANES), lambda i: (i, 0)),
            pl.BlockSpec((tm, LANES), lambda i: (i, 0)),
        ],
        out_specs=[
            pl.BlockSpec((tm, MLA_HEADS * MLA_QK_PAD), lambda i: (i, 0)),
            pl.BlockSpec((tm, MLA_HEADS * MLA_QK_PAD), lambda i: (i, 0)),
        ],
        out_shape=[
            jax.ShapeDtypeStruct((t, MLA_HEADS * MLA_QK_PAD), BF16),
            jax.ShapeDtypeStruct((t, MLA_HEADS * MLA_QK_PAD), BF16),
        ],
        compiler_params=_cparams("arbitrary"),
        name="mla_kv",
    )(p, p, gain.reshape(1, lora), w_k, w_v, cos, sin)


def _flash_kernel(q_ref, k_ref, v_ref, o_ref, m_ref, acc_ref, *, tk):
    m_ref[...] = jnp.full(m_ref.shape, -jnp.inf, F32)
    acc_ref[...] = jnp.zeros(acc_ref.shape, F32)
    q = q_ref[...]

    def body(i, _):
        rows = pl.ds(pl.multiple_of(i * tk, tk), tk)
        s = lax.dot_general(q, k_ref[rows, :], (((1,), (1,)), ((), ())), preferred_element_type=F32)
        m_prev = m_ref[...]
        m_new = jnp.maximum(m_prev, jnp.max(s, axis=-1, keepdims=True))
        p = jnp.exp2(s - m_new).astype(BF16)
        acc_ref[...] = jnp.exp2(m_prev - m_new) * acc_ref[...] + jnp.dot(p, v_ref[rows, :],
                                                                         preferred_element_type=F32)
        m_ref[...] = m_new
        return 0

    lax.fori_loop(0, k_ref.shape[0] // tk, body, 0, unroll=True)
    acc = acc_ref[...]
    o_ref[...] = (acc[:, :MLA_V] / acc[:, MLA_V:MLA_V + 1]).astype(o_ref.dtype)


def flash_attention(q, k, v, bsz, seq, *, tq=512, tk=1024):
    t = q.shape[0]
    tq, tk = min(tq, seq), min(tk, seq)
    nq = seq // tq
    return pl.pallas_call(
        functools.partial(_flash_kernel, tk=tk),
        grid=(bsz, MLA_HEADS, nq),
        in_specs=[
            pl.BlockSpec((tq, MLA_QK_PAD), lambda b, h, qi: (b * nq + qi, h)),
            pl.BlockSpec((seq, MLA_QK_PAD), lambda b, h, qi: (b, h)),
            pl.BlockSpec((seq, MLA_QK_PAD), lambda b, h, qi: (b, h)),
        ],
        out_specs=pl.BlockSpec((tq, MLA_V), lambda b, h, qi: (b * nq + qi, h)),
        out_shape=jax.ShapeDtypeStruct((t, MLA_HEADS * MLA_V), BF16),
        scratch_shapes=[pltpu.VMEM((tq, 1), F32), pltpu.VMEM((tq, MLA_QK_PAD), F32)],
        compiler_params=_cparams("arbitrary", "arbitrary", "arbitrary"),
        name="flash_attention",
    )(q, k, v)


def _conv4(main_ref, prev_ref, next_ref, w_ref, pad_ref, first, last):
    tm = main_ref.shape[0]
    x = main_ref[...]
    pad_ref[0:SUBLANES, :] = jnp.where(first, 0.0, prev_ref[...])
    pad_ref[SUBLANES:SUBLANES + tm, :] = x
    pad_ref[SUBLANES + tm:2 * SUBLANES + tm, :] = jnp.where(last, 0.0, next_ref[...])
    w = w_ref[...]
    return (w[0:1] * pad_ref[SUBLANES - 1:SUBLANES - 1 + tm, :] + w[1:2] * x
            + w[2:3] * pad_ref[SUBLANES + 1:SUBLANES + 1 + tm, :]
            + w[3:4] * pad_ref[SUBLANES + 2:SUBLANES + 2 + tm, :])


def _halo_specs(tm, width, col, n_row8, grid_rank):
    r8 = tm // SUBLANES
    if grid_rank == 1:
        return [
            pl.BlockSpec((tm, width), lambda i: (i, col)),
            pl.BlockSpec((SUBLANES, width), lambda i: (jnp.maximum(i * r8 - 1, 0), col)),
            pl.BlockSpec((SUBLANES, width), lambda i: (jnp.minimum((i + 1) * r8, n_row8 - 1), col)),
        ]
    return [
        pl.BlockSpec((tm, width), lambda i, j: (i, col + j)),
        pl.BlockSpec((SUBLANES, width), lambda i, j: (jnp.maximum(i * r8 - 1, 0), col + j)),
        pl.BlockSpec((SUBLANES, width), lambda i, j: (jnp.minimum((i + 1) * r8, n_row8 - 1), col + j)),
    ]


def _gdn_prep_kernel(main_ref, prev_ref, next_ref, w_ref, o_ref, pad_ref, *, tiles_per_batch):
    i, j = pl.program_id(0), pl.program_id(1)
    first = (i % tiles_per_batch) == 0
    last = (i % tiles_per_batch) == tiles_per_batch - 1
    y = _silu(_conv4(main_ref, prev_ref, next_ref, w_ref, pad_ref, first, last))

    @pl.when(j < 2)
    def _():
        post = jnp.where(j == 0, GDN_DIM ** -0.5, 1.0)
        for h in range(GDN_HEADS):
            seg = y[:, h * GDN_DIM:(h + 1) * GDN_DIM]
            inv = lax.rsqrt(jnp.sum(seg * seg, axis=-1, keepdims=True) + NORM_EPS)
            o_ref[:, h * GDN_DIM:(h + 1) * GDN_DIM] = seg * (inv * post)

    @pl.when(j == 2)
    def _():
        o_ref[...] = y


def gdn_prep(p, conv_w, seq, *, tm=512):
    t = p.shape[0]
    width = GDN_HEADS * GDN_DIM
    col = P_QKV * LANES // width
    return pl.pallas_call(
        functools.partial(_gdn_prep_kernel, tiles_per_batch=seq // tm),
        grid=(t // tm, 3),
        in_specs=_halo_specs(tm, width, col, t // SUBLANES, 2) + [pl.BlockSpec((4, width), lambda i, j: (0, j))],
        out_specs=pl.BlockSpec((tm, width), lambda i, j: (i, j)),
        out_shape=jax.ShapeDtypeStruct((t, 3 * width), F32),
        scratch_shapes=[pltpu.VMEM((tm + 2 * SUBLANES, width), F32)],
        compiler_params=_cparams("arbitrary", "arbitrary"),
        name="gdn_prep",
    )(p, p, p, conv_w)


def _unit_triangular_inverse(low, dot):
    c = low[0].shape[0]
    heads = range(len(low))
    row = lax.broadcasted_iota(jnp.int32, (c, c), 0)
    col = lax.broadcasted_iota(jnp.int32, (c, c), 1)
    eye = (row == col).astype(F32)
    inv = [eye - jnp.where((row ^ 1) == col, low[h], 0.0) for h in heads]
    shift = 1
    while (2 << shift) <= c:
        off_block = ((row >> shift) ^ 1) == (col >> shift)
        blk = [jnp.where(off_block, low[h], 0.0) for h in heads]
        half = [dot(inv[h], blk[h]) for h in heads]
        inv = [inv[h] - dot(half[h], inv[h]) for h in heads]
        shift += 1
    return inv


def _gdn_chunk_kernel(qf_ref, kf_ref, vf_ref, abf_ref, qr_ref, kr_ref, vr_ref, abr_ref, alog_ref, dt_ref,
                      of_ref, or_ref, state_ref):
    c = qf_ref.shape[0]

    @pl.when(pl.program_id(1) == 0)
    def _():
        state_ref[...] = jnp.zeros(state_ref.shape, F32)

    row = lax.broadcasted_iota(jnp.int32, (c, c), 0)
    col = lax.broadcasted_iota(jnp.int32, (c, c), 1)
    eye_l = (lax.broadcasted_iota(jnp.int32, (LANES, LANES), 0)
             == lax.broadcasted_iota(jnp.int32, (LANES, LANES), 1)).astype(F32)
    sl = [slice(h * GDN_DIM, (h + 1) * GDN_DIM) for h in range(GDN_HEADS)]

    q, k, v, g_col, g_row, beta, incl, strict, g_last = [], [], [], [], [], [], [], [], []
    for reverse, (q_ref, k_ref, v_ref, ab_ref) in enumerate(((qf_ref, kf_ref, vf_ref, abf_ref),
                                                              (qr_ref, kr_ref, vr_ref, abr_ref))):
        ab = ab_ref[...]
        log_alpha = -jnp.exp(alog_ref[...]) * _softplus(ab + dt_ref[...])
        beta_all = jax.nn.sigmoid(ab)
        cum = ((col >= row) if reverse else (col <= row)).astype(F32)
        g_all = _fdot(cum, log_alpha)
        g_rows = lax.dot_general(eye_l, g_all, (((1,), (1,)), ((), ())), preferred_element_type=F32,
                                 precision=HIGHEST)
        a_lane = 2 * GDN_HEADS * reverse
        b_lane = a_lane + GDN_HEADS
        last = 0 if reverse else c - 1
        for h in range(GDN_HEADS):
            q.append(q_ref[:, sl[h]])
            k.append(k_ref[:, sl[h]])
            v.append(v_ref[:, sl[h]])
            g_col.append(g_all[:, a_lane + h:a_lane + h + 1])
            g_row.append(g_rows[a_lane + h:a_lane + h + 1, :])
            beta.append(beta_all[:, b_lane + h:b_lane + h + 1])
            incl.append((row <= col) if reverse else (row >= col))
            strict.append((row < col) if reverse else (row > col))
            g_last.append(g_all[last:last + 1, a_lane + h:a_lane + h + 1])

    chains = range(2 * GDN_HEADS)
    state = [state_ref[n] for n in chains]
    decay = [jnp.exp(jnp.where(incl[n], g_col[n] - g_row[n], -jnp.inf)) for n in chains]
    eg = [jnp.exp(g_col[n]) for n in chains]
    kb = [k[n] * beta[n] for n in chains]
    low = [jnp.where(strict[n], _bdot_nt(kb[n], k[n]) * decay[n], 0.0) for n in chains]
    qk = [_bdot_nt(q[n], k[n]) * decay[n] for n in chains]
    inv = _unit_triangular_inverse(low, _bdot)
    sol = [_bdot(inv[n], jnp.concatenate([v[n] * beta[n], kb[n] * eg[n]], axis=1)) for n in chains]
    v_new = [sol[n][:, :GDN_DIM] - _bdot(sol[n][:, GDN_DIM:], state[n]) for n in chains]
    out = [_bdot(q[n] * eg[n], state[n]) + _bdot(qk[n], v_new[n]) for n in chains]
    new_state = [state[n] * jnp.exp(g_last[n]) + _bdot_tn(k[n] * jnp.exp(g_last[n] - g_col[n]), v_new[n])
                 for n in chains]
    for h in range(GDN_HEADS):
        of_ref[:, sl[h]] = out[h]
        or_ref[:, sl[h]] = out[GDN_HEADS + h]
    for n in chains:
        state_ref[n] = new_state[n]


def gdn_chunk_scan(qkv, p, a_log_row, dt_row, bsz, seq):
    t = qkv.shape[0]
    c = GDN_CHUNK
    n = seq // c
    width = GDN_HEADS * GDN_DIM
    fwd = lambda b, i: b * n + i
    rev = lambda b, i: b * n + n - 1 - i
    chunk = lambda rows, col: pl.BlockSpec((c, width), lambda b, i: (rows(b, i), col))
    gates = lambda rows: pl.BlockSpec((c, LANES), lambda b, i: (rows(b, i), P_AB))
    param = pl.BlockSpec((1, LANES), lambda b, i: (0, 0))
    return pl.pallas_call(
        _gdn_chunk_kernel,
        grid=(bsz, n),
        in_specs=[chunk(fwd, 0), chunk(fwd, 1), chunk(fwd, 2), gates(fwd),
                  chunk(rev, 0), chunk(rev, 1), chunk(rev, 2), gates(rev), param, param],
        out_specs=[chunk(fwd, 0), chunk(rev, 0)],
        out_shape=[jax.ShapeDtypeStruct((t, width), F32)] * 2,
        scratch_shapes=[pltpu.VMEM((2 * GDN_HEADS, GDN_DIM, GDN_DIM), F32)],
        compiler_params=_cparams("arbitrary", "arbitrary"),
        name="gdn_chunk_scan",
    )(qkv, qkv, qkv, p, qkv, qkv, qkv, p, a_log_row, dt_row)


def _gdn_out_kernel(of_ref, ob_ref, z_ref, gain_ref, o_ref):
    o = of_ref[...] + ob_ref[...]
    z = z_ref[...]
    gain = gain_ref[...]
    for h in range(GDN_HEADS):
        sl = slice(h * GDN_DIM, (h + 1) * GDN_DIM)
        o_ref[:, sl] = (_rms(o[:, sl], gain) * _silu(z[:, sl])).astype(BF16)


def gdn_output(o_f, o_b, p, out_norm, *, tm=512):
    t, width = o_f.shape
    return pl.pallas_call(
        _gdn_out_kernel,
        grid=(t // tm,),
        in_specs=[
            pl.BlockSpec((tm, width), lambda i: (i, 0)),
            pl.BlockSpec((tm, width), lambda i: (i, 0)),
            pl.BlockSpec((tm, width), lambda i: (i, P_Z * LANES // width)),
            pl.BlockSpec((1, GDN_DIM), lambda i: (0, 0)),
        ],
        out_specs=pl.BlockSpec((tm, width), lambda i: (i, 0)),
        out_shape=jax.ShapeDtypeStruct((t, width), BF16),
        compiler_params=_cparams("arbitrary"),
        name="gdn_output",
    )(o_f, o_b, p, out_norm.reshape(1, GDN_DIM))


def _lru_scan_tile(a, u, carry_ref, a_ref, b_ref, h_ref, reverse):
    tm = a.shape[0]
    sub = lax.broadcasted_iota(jnp.int32, a.shape, 0) % SUBLANES
    shift = 1
    while shift < SUBLANES:
        if reverse:
            ok = sub < SUBLANES - shift
            a_sh, u_sh = pltpu.roll(a, tm - shift, 0), pltpu.roll(u, tm - shift, 0)
        else:
            ok = sub >= shift
            a_sh, u_sh = pltpu.roll(a, shift, 0), pltpu.roll(u, shift, 0)
        u = a * jnp.where(ok, u_sh, 0.0) + u
        a = a * jnp.where(ok, a_sh, 1.0)
        shift *= 2
    a_ref[...] = a
    b_ref[...] = u
    groups = tm // SUBLANES
    edge = 0 if reverse else SUBLANES - 1

    def body(g, h_in):
        r0 = pl.multiple_of((groups - 1 - g if reverse else g) * SUBLANES, SUBLANES)
        h = a_ref[pl.ds(r0, SUBLANES), :] * h_in + b_ref[pl.ds(r0, SUBLANES), :]
        h_ref[pl.ds(r0, SUBLANES), :] = h
        return jnp.broadcast_to(h[edge:edge + 1, :], h.shape)

    carry_ref[...] = lax.fori_loop(0, groups, body, carry_ref[...], unroll=4)


def _lru_gates(xc, w_ref, ba_ref, bi_ref, lam_ref):
    blk = xc.shape[1] // LRU_BLOCKS
    parts = [_bdot(xc[:, n * blk:(n + 1) * blk], w_ref[n]) for n in range(LRU_BLOCKS)]
    r = jax.nn.sigmoid(jnp.concatenate([p[:, :blk] for p in parts], axis=1) + ba_ref[...])
    gate_i = jax.nn.sigmoid(jnp.concatenate([p[:, blk:] for p in parts], axis=1) + bi_ref[...])
    log_a = -LRU_C * r * _softplus(-lam_ref[...])
    a = jnp.exp(log_a)
    return a, xc * gate_i * jnp.sqrt(1.0 - a * a)


def _lru_fwd_kernel(main_ref, prev_ref, next_ref, cw_ref, cb_ref, w_ref, ba_ref, bi_ref, lam_ref,
                    h_ref, pad_ref, carry_ref, a_ref, b_ref, *, tiles_per_batch):
    i = pl.program_id(0) % tiles_per_batch

    @pl.when(i == 0)
    def _():
        carry_ref[...] = jnp.zeros(carry_ref.shape, F32)

    xc = _conv4(main_ref, prev_ref, next_ref, cw_ref, pad_ref, i == 0, i == tiles_per_batch - 1) + cb_ref[...]
    a, u = _lru_gates(xc, w_ref, ba_ref, bi_ref, lam_ref)
    _lru_scan_tile(a, u, carry_ref, a_ref, b_ref, h_ref, False)


def _lru_bwd_kernel(main_ref, prev_ref, next_ref, cw_ref, cb_ref, w_ref, ba_ref, bi_ref, lam_ref, hf_ref, y_ref,
                    o_ref, pad_ref, carry_ref, a_ref, b_ref, h_ref, *, tiles_per_batch):
    i = pl.program_id(0) % tiles_per_batch

    @pl.when(i == 0)
    def _():
        carry_ref[...] = jnp.zeros(carry_ref.shape, F32)

    xc = _conv4(main_ref, prev_ref, next_ref, cw_ref, pad_ref, i == tiles_per_batch - 1, i == 0) + cb_ref[...]
    a, u = _lru_gates(xc, w_ref, ba_ref, bi_ref, lam_ref)
    _lru_scan_tile(a, u, carry_ref, a_ref, b_ref, h_ref, True)
    o_ref[...] = ((hf_ref[...] + h_ref[...]) * jax.nn.gelu(y_ref[...], approximate=True)).astype(BF16)


def _lru_param_specs(width, blk):
    return [
        pl.BlockSpec((4, width), lambda i: (0, 0)),
        pl.BlockSpec((1, width), lambda i: (0, 0)),
        pl.BlockSpec((LRU_BLOCKS, blk, 2 * blk), lambda i: (0, 0, 0)),
        pl.BlockSpec((1, width), lambda i: (0, 0)),
        pl.BlockSpec((1, width), lambda i: (0, 0)),
        pl.BlockSpec((1, width), lambda i: (0, 0)),
    ]


def _lru_scratch(tm, width):
    return [pltpu.VMEM((tm + 2 * SUBLANES, width), F32), pltpu.VMEM((SUBLANES, width), F32),
            pltpu.VMEM((tm, width), F32), pltpu.VMEM((tm, width), F32)]


def rg_lru(p, conv_w, conv_b, w_a, b_a, w_i, b_i, lam, seq, *, tm=256):
    t = p.shape[0]
    width = conv_w.shape[1]
    blk = width // LRU_BLOCKS
    n_tiles = t // tm
    col_x, col_y = P_LRUX * LANES // width, P_LRUY * LANES // width
    w_dir = [jnp.concatenate([w_a[d], w_i[d]], axis=-1).astype(BF16) for d in range(2)]
    params = lambda d: (conv_w, conv_b.reshape(1, width), w_dir[d], b_a[d].reshape(1, width),
                        b_i[d].reshape(1, width), lam[d].reshape(1, width))
    h_f = pl.pallas_call(
        functools.partial(_lru_fwd_kernel, tiles_per_batch=seq // tm),
        grid=(n_tiles,),
        in_specs=_halo_specs(tm, width, col_x, t // SUBLANES, 1) + _lru_param_specs(width, blk),
        out_specs=pl.BlockSpec((tm, width), lambda i: (i, 0)),
        out_shape=jax.ShapeDtypeStruct((t, width), F32),
        scratch_shapes=_lru_scratch(tm, width),
        compiler_params=_cparams("arbitrary"),
        name="rg_lru_fwd",
    )(p, p, p, *params(0))
    r8 = tm // SUBLANES
    n_row8 = t // SUBLANES
    rev = lambda i: n_tiles - 1 - i
    halo_rev = [
        pl.BlockSpec((tm, width), lambda i: (rev(i), col_x)),
        pl.BlockSpec((SUBLANES, width), lambda i: (jnp.maximum(rev(i) * r8 - 1, 0), col_x)),
        pl.BlockSpec((SUBLANES, width), lambda i: (jnp.minimum((rev(i) + 1) * r8, n_row8 - 1), col_x)),
    ]
    return pl.pallas_call(
        functools.partial(_lru_bwd_kernel, tiles_per_batch=seq // tm),
        grid=(n_tiles,),
        in_specs=halo_rev + _lru_param_specs(width, blk) + [
            pl.BlockSpec((tm, width), lambda i: (rev(i), 0)),
            pl.BlockSpec((tm, width), lambda i: (rev(i), col_y)),
        ],
        out_specs=pl.BlockSpec((tm, width), lambda i: (rev(i), 0)),
        out_shape=jax.ShapeDtypeStruct((t, width), BF16),
        scratch_shapes=_lru_scratch(tm, width) + [pltpu.VMEM((tm, width), F32)],
        compiler_params=_cparams("arbitrary"),
        name="rg_lru_bwd",
    )(p, p, p, *params(1), h_f, p)


def _merge_kernel(oa_ref, ob_ref, oc_ref, wa_ref, wb_ref, wc_ref, ga_ref, gb_ref, gc_ref, o_ref):
    acc = jax.nn.sigmoid(ga_ref[...]) * jnp.dot(oa_ref[...], wa_ref[...], preferred_element_type=F32)
    acc += jax.nn.sigmoid(gb_ref[...]) * jnp.dot(ob_ref[...], wb_ref[...], preferred_element_type=F32)
    acc += jax.nn.sigmoid(gc_ref[...]) * jnp.dot(oc_ref[...], wc_ref[...], preferred_element_type=F32)
    o_ref[...] = acc.astype(BF16)


def merge_branches(o_a, o_b, o_c, w_a, w_b, w_c, p, *, tm=256, tn=2048):
    t = o_a.shape[0]
    d = w_a.shape[1]
    n_col = d // tn
    branch = lambda arr: pl.BlockSpec((tm, arr.shape[1]), lambda j, i: (i, 0))
    weight = lambda arr: pl.BlockSpec((arr.shape[0], tn), lambda j, i: (0, j), pipeline_mode=pl.Buffered(1))
    gate = lambda g: pl.BlockSpec((tm, tn), lambda j, i: (i, P_GATE * LANES // tn + g * n_col + j))
    return pl.pallas_call(
        _merge_kernel,
        grid=(n_col, t // tm),
        in_specs=[branch(o_a), branch(o_b), branch(o_c), weight(w_a), weight(w_b), weight(w_c),
                  gate(0), gate(1), gate(2)],
        out_specs=pl.BlockSpec((tm, tn), lambda j, i: (i, j)),
        out_shape=jax.ShapeDtypeStruct((t, d), BF16),
        compiler_params=_cparams("arbitrary", "arbitrary"),
        name="merge_branches",
    )(o_a, o_b, o_c, w_a, w_b, w_c, p, p, p)


def _residual_kernel(a_ref, w_ref, x_ref, g_ref, o_ref):
    o_ref[...] = x_ref[...] + g_ref[...] * jnp.dot(a_ref[...], w_ref[...], preferred_element_type=F32)


def matmul_gated_residual(a, w, x, gate, seq, *, tm=512, tn=512):
    t, k = a.shape
    d = w.shape[1]
    tm = min(tm, seq)
    tiles_per_batch = seq // tm
    return pl.pallas_call(
        _residual_kernel,
        grid=(t // tm, d // tn),
        in_specs=[
            pl.BlockSpec((tm, k), lambda i, j: (i, 0)),
            pl.BlockSpec((k, tn), lambda i, j: (0, j)),
            pl.BlockSpec((tm, tn), lambda i, j: (i, j)),
            pl.BlockSpec((None, 1, tn), lambda i, j: (i // tiles_per_batch, 0, j)),
        ],
        out_specs=pl.BlockSpec((tm, tn), lambda i, j: (i, j)),
        out_shape=jax.ShapeDtypeStruct((t, d), F32),
        compiler_params=_cparams("arbitrary", "arbitrary"),
        name="matmul_gated_residual",
    )(a, w, x, gate)


def _swiglu_up_kernel(x_ref, gain_ref, scale_ref, shift_ref, w1_ref, w3_ref, o_ref, h_ref):
    @pl.when(pl.program_id(1) == 0)
    def _():
        _store_norm_mod(x_ref, gain_ref, scale_ref, shift_ref, h_ref)

    h = h_ref[...]
    up = jnp.dot(h, w1_ref[...], preferred_element_type=F32)
    o_ref[...] = (_silu(up) * jnp.dot(h, w3_ref[...], preferred_element_type=F32)).astype(BF16)


def norm_swiglu_up(x, gain, scale, shift, w1, w3, seq, *, tm=1024, tn=512):
    t, d = x.shape
    f = w1.shape[1]
    tm = min(tm, seq)
    return pl.pallas_call(
        _swiglu_up_kernel,
        grid=(t // tm, f // tn),
        in_specs=_row_specs(d, tm, seq // tm) + [pl.BlockSpec((d, tn), lambda i, j: (0, j))] * 2,
        out_specs=pl.BlockSpec((tm, tn), lambda i, j: (i, j)),
        out_shape=jax.ShapeDtypeStruct((t, f), BF16),
        scratch_shapes=[pltpu.VMEM((tm, d), BF16)],
        compiler_params=_cparams("arbitrary", "arbitrary"),
        name="norm_swiglu_up",
    )(x, gain.reshape(1, d), scale, shift, w1, w3)


def _router_kernel(x_ref, gain_ref, scale_ref, shift_ref, w_ref, b_ref, h_ref, route_ref):
    h = _norm_mod(x_ref[...], gain_ref[...], scale_ref[...], shift_ref[...])
    h_ref[...] = h
    lane =lax.broadcasted_iota(jnp.int32, route_ref.shape, 1)
    logits = jnp.where(lane < N_EXPERTS, _fdot(h, w_ref[...]) + b_ref[...], -jnp.inf)
    top1 = jnp.max(logits, axis=-1, keepdims=True)
    idx1 = jnp.min(jnp.where(logits == top1, lane, LANES), axis=-1, keepdims=True)
    rest = jnp.where(lane == idx1, -jnp.inf, logits)
    top2 = jnp.max(rest, axis=-1, keepdims=True)
    idx2 = jnp.min(jnp.where(rest == top2, lane, LANES), axis=-1, keepdims=True)
    e2 = jnp.exp(top2 - top1)
    inv = 1.0 / (1.0 + e2)
    route_ref[...] = jnp.where(lane == 0, idx1.astype(F32),
                               jnp.where(lane == 1, idx2.astype(F32),
                                         jnp.where(lane == 2, inv, jnp.where(lane == 3, e2 * inv, 0.0))))


def norm_router(x, gain, scale, shift, w_router, b_router, seq, *, tm=512):
    t, d = x.shape
    w_pad = jnp.zeros((d, LANES), F32).at[:, :N_EXPERTS].set(w_router)
    b_pad = jnp.zeros((1, LANES), F32).at[0, :N_EXPERTS].set(b_router)
    tiles_per_batch = seq // tm
    return pl.pallas_call(
        _router_kernel,
        grid=(t // tm,),
        in_specs=[
            pl.BlockSpec((tm, d), lambda i: (i, 0)),
            pl.BlockSpec((1, d), lambda i: (0, 0)),
            pl.BlockSpec((None, 1, d), lambda i: (i // tiles_per_batch, 0, 0)),
            pl.BlockSpec((None, 1, d), lambda i: (i // tiles_per_batch, 0, 0)),
            pl.BlockSpec((d, LANES), lambda i: (0, 0)),
            pl.BlockSpec((1, LANES), lambda i: (0, 0)),
        ],
        out_specs=[pl.BlockSpec((tm, d), lambda i: (i, 0)), pl.BlockSpec((tm, LANES), lambda i: (i, 0))],
        out_shape=[jax.ShapeDtypeStruct((t, d), F32), jax.ShapeDtypeStruct((t, LANES), F32)],
        compiler_params=_cparams("arbitrary"),
        name="norm_router",
    )(x, gain.reshape(1, d), scale, shift, w_pad, b_pad)


def _gather_rows(src_hbm, jobs, sem_idx, sem_rows):
    for idx_ref, idx_smem, _ in jobs:
        load = pltpu.make_async_copy(idx_ref, idx_smem, sem_idx)
        load.start()
        load.wait()

    def row_copy(buf_ref, r, src_row):
        return pltpu.make_async_copy(src_hbm.at[pl.ds(src_row, 1)], buf_ref.at[pl.ds(r, 1)], sem_rows)

    for _, idx_smem, buf_ref in jobs:
        def issue(r, carry, idx_smem=idx_smem, buf_ref=buf_ref):
            row_copy(buf_ref, r, idx_smem[0, 0, r]).start()
            return carry

        lax.fori_loop(0, buf_ref.shape[0], issue, 0, unroll=8)
    for _, _, buf_ref in jobs:
        def drain(r, carry, buf_ref=buf_ref):
            row_copy(buf_ref, r, 0).wait()
            return carry

        lax.fori_loop(0, buf_ref.shape[0], drain, 0, unroll=8)


def _dispatch_kernel(used_ref, idx_ref, h_hbm, o_ref, buf_ref, idx_smem, sem_idx, sem_rows):
    live = pl.program_id(0) < used_ref[0]

    @pl.when(live)
    def _():
        _gather_rows(h_hbm, [(idx_ref, idx_smem, buf_ref)], sem_idx, sem_rows)
        o_ref[...] = buf_ref[...].astype(BF16)

    @pl.when(jnp.logical_not(live))
    def _():
        o_ref[...] = jnp.zeros(o_ref.shape, BF16)


def expert_dispatch(n_used, row_tok, h):
    d = h.shape[1]
    n_blocks = row_tok.shape[0] // MOE_BLOCK
    return pl.pallas_call(
        _dispatch_kernel,
        grid_spec=pltpu.PrefetchScalarGridSpec(
            num_scalar_prefetch=1,
            grid=(n_blocks,),
            in_specs=[
                pl.BlockSpec((1, 1, MOE_BLOCK), lambda i, used: (i, 0, 0)),
                pl.BlockSpec(memory_space=pl.ANY),
            ],
            out_specs=pl.BlockSpec((MOE_BLOCK, d), lambda i, used: (i, 0)),
            scratch_shapes=[pltpu.VMEM((MOE_BLOCK, d), F32), pltpu.SMEM((1, 1, MOE_BLOCK), jnp.int32),
                            pltpu.SemaphoreType.DMA(()), pltpu.SemaphoreType.DMA(())],
        ),
        out_shape=jax.ShapeDtypeStruct((n_blocks * MOE_BLOCK, d), BF16),
        compiler_params=_cparams("arbitrary"),
        name="expert_dispatch",
    )(n_used, row_tok.reshape(n_blocks, 1, MOE_BLOCK), h)


def _expert_up_kernel(be_ref, used_ref, x_ref, w1_ref, w3_ref, o_ref, w1b_ref, w3b_ref):
    i = pl.program_id(1)

    @pl.when(jnp.logical_or(i == 0, be_ref[i] != be_ref[jnp.maximum(i - 1, 0)]))
    def _():
        w1b_ref[...] = w1_ref[...].astype(BF16)
        w3b_ref[...] = w3_ref[...].astype(BF16)

    @pl.when(i < used_ref[0])
    def _():
        x = x_ref[...]
        up = jnp.dot(x, w1b_ref[...], preferred_element_type=F32)
        o_ref[...] = (_silu(up) * jnp.dot(x, w3b_ref[...], preferred_element_type=F32)).astype(BF16)

    @pl.when(i >= used_ref[0])
    def _():
        o_ref[...] = jnp.zeros(o_ref.shape, BF16)


def expert_swiglu_up(block_e, n_used, xs, w1, w3, *, tn=512):
    rows, d = xs.shape
    f = w1.shape[2]
    return pl.pallas_call(
        _expert_up_kernel,
        grid_spec=pltpu.PrefetchScalarGridSpec(
            num_scalar_prefetch=2,
            grid=(f // tn, rows // MOE_BLOCK),
            in_specs=[
                pl.BlockSpec((MOE_BLOCK, d), lambda j, i, be, used: (i, 0)),
                pl.BlockSpec((None, d, tn), lambda j, i, be, used: (be[i], 0, j)),
                pl.BlockSpec((None, d, tn), lambda j, i, be, used: (be[i], 0, j)),
            ],
            out_specs=pl.BlockSpec((MOE_BLOCK, tn), lambda j, i, be, used: (i, j)),
            scratch_shapes=[pltpu.VMEM((d, tn), BF16), pltpu.VMEM((d, tn), BF16)],
        ),
        out_shape=jax.ShapeDtypeStruct((rows, f), BF16),
        compiler_params=_cparams("arbitrary", "arbitrary"),
        name="expert_swiglu_up",
    )(block_e, n_used, xs, w1, w3)


def _expert_down_kernel(be_ref, used_ref, a_ref, w_ref, o_ref):
    i = pl.program_id(1)

    @pl.when(i < used_ref[0])
    def _():
        o_ref[...] = jnp.dot(a_ref[...], w_ref[...], preferred_element_type=F32)

    @pl.when(i >= used_ref[0])
    def _():
        o_ref[...] = jnp.zeros(o_ref.shape, F32)


def expert_down(block_e, n_used, act, w2, *, tn=1024):
    rows, f = act.shape
    d = w2.shape[2]
    return pl.pallas_call(
        _expert_down_kernel,
        grid_spec=pltpu.PrefetchScalarGridSpec(
            num_scalar_prefetch=2,
            grid=(d // tn, rows // MOE_BLOCK),
            in_specs=[
                pl.BlockSpec((MOE_BLOCK, f), lambda j, i, be, used: (i, 0)),
                pl.BlockSpec((None, f, tn), lambda j, i, be, used: (be[i], 0, j)),
            ],
            out_specs=pl.BlockSpec((MOE_BLOCK, tn), lambda j, i, be, used: (i, j)),
        ),
        out_shape=jax.ShapeDtypeStruct((rows, d), F32),
        compiler_params=_cparams("arbitrary", "arbitrary"),
        name="expert_down",
    )(block_e, n_used, act, w2)


def _combine_kernel(idx1_ref, idx2_ref, x_ref, route_ref, g_ref, y_hbm, o_ref,
                    buf1_ref, buf2_ref, idx1_smem, idx2_smem, sem_idx, sem_rows):
    _gather_rows(y_hbm, [(idx1_ref, idx1_smem, buf1_ref), (idx2_ref, idx2_smem, buf2_ref)], sem_idx, sem_rows)
    route = route_ref[...]
    y = route[:, 2:3] * buf1_ref[...] + route[:, 3:4] * buf2_ref[...]
    o_ref[...] = x_ref[...] + g_ref[...] * y


def combine_residual(x, y, dest, route, gate, seq, *, tm=256):
    t, d = x.shape
    tiles_per_batch = seq // tm
    n_tiles = t // tm
    row = pl.BlockSpec((tm, d), lambda i: (i, 0))
    idx = pl.BlockSpec((1, 1, tm), lambda i: (i, 0, 0))
    return pl.pallas_call(
        _combine_kernel,
        grid=(n_tiles,),
        in_specs=[idx, idx, row, pl.BlockSpec((tm, LANES), lambda i: (i, 0)),
                  pl.BlockSpec((None, 1, d), lambda i: (i // tiles_per_batch, 0, 0)),
                  pl.BlockSpec(memory_space=pl.ANY)],
        out_specs=row,
        out_shape=jax.ShapeDtypeStruct((t, d), F32),
        scratch_shapes=[pltpu.VMEM((tm, d), F32), pltpu.VMEM((tm, d), F32),
                        pltpu.SMEM((1, 1, tm), jnp.int32), pltpu.SMEM((1, 1, tm), jnp.int32),
                        pltpu.SemaphoreType.DMA(()), pltpu.SemaphoreType.DMA(())],
        compiler_params=_cparams("arbitrary"),
        name="combine_residual",
    )(dest[:, 0].reshape(n_tiles, 1, tm), dest[:, 1].reshape(n_tiles, 1, tm), x, route, gate, y)


def routing_plan(route, n_tokens):
    n_pairs = n_tokens * TOP_K
    flat_e = route[:, :TOP_K].astype(jnp.int32).reshape(n_pairs)
    onehot = (flat_e[:, None] == jnp.arange(N_EXPERTS, dtype=jnp.int32)[None, :]).astype(jnp.int32)
    running = jnp.cumsum(onehot, axis=0)
    counts = running[-1]
    rank = jnp.take_along_axis(running, flat_e[:, None], axis=1)[:, 0] - 1
    padded = (counts + MOE_BLOCK - 1) // MOE_BLOCK * MOE_BLOCK
    pend = jnp.cumsum(padded)
    dest = (pend - padded)[flat_e] + rank
    n_blocks = -(-n_pairs // MOE_BLOCK) + N_EXPERTS
    flat_tok = jnp.repeat(jnp.arange(n_tokens, dtype=jnp.int32), TOP_K)
    row_tok = jnp.zeros((n_blocks * MOE_BLOCK,), jnp.int32).at[dest].set(flat_tok)
    block_e = jnp.minimum(jnp.searchsorted(pend, jnp.arange(n_blocks, dtype=jnp.int32) * MOE_BLOCK, side="right"),
                          N_EXPERTS - 1).astype(jnp.int32)
    n_used = (pend[-1:] // MOE_BLOCK).astype(jnp.int32)
    return row_tok, dest.reshape(n_tokens, TOP_K), block_e, n_used


def _final_norm_kernel(x_ref, gain_ref, o_ref):
    o_ref[...] = _rms(x_ref[...], gain_ref[...])


def final_rms_norm(x, gain, *, tm=512):
    t, d = x.shape
    return pl.pallas_call(
        _final_norm_kernel,
        grid=(t // tm,),
        in_specs=[pl.BlockSpec((tm, d), lambda i: (i, 0)), pl.BlockSpec((1, d), lambda i: (0, 0))],
        out_specs=pl.BlockSpec((tm, d), lambda i: (i, 0)),
        out_shape=jax.ShapeDtypeStruct((t, d), F32),
        compiler_params=_cparams("arbitrary"),
        name="final_norm",
    )(x, gain.reshape(1, d))


def _pad_cols(w, width):
    return jnp.pad(w, ((0, 0), (0, width - w.shape[1])))


def projection_weight(w_in, w_gate):
    lru_width = (w_in.shape[1] - IN_FIXED_WIDTH) // 2
    o = np.cumsum((0,) + IN_FIXED_WIDTHS + (lru_width, lru_width))
    c_q, c_kv, k_rope, qkv, z, ab, lru_x, lru_y = (w_in[:, int(o[k]):int(o[k + 1])] for k in range(8))
    parts = [w_gate, qkv, z, lru_x, lru_y, c_q, c_kv, _pad_cols(k_rope, LANES), _pad_cols(ab, LANES)]
    return jnp.concatenate(parts, axis=1).astype(BF16)


def mla_weights(w_uq, w_ukv):
    d_q, d_kv = w_uq.shape[0], w_ukv.shape[0]
    wq = w_uq.reshape(d_q, MLA_HEADS, MLA_NOPE + MLA_ROPE)
    wq = jnp.pad(wq, ((0, 0), (0, 0), (0, MLA_QK_PAD - MLA_NOPE - MLA_ROPE))).reshape(d_q, MLA_HEADS * MLA_QK_PAD)
    wkv = w_ukv.reshape(d_kv, MLA_HEADS, MLA_NOPE + MLA_V)
    wk = wkv[:, :, :MLA_NOPE].reshape(d_kv, MLA_HEADS * MLA_NOPE)
    wv = wkv[:, :, MLA_NOPE:].reshape(d_kv, MLA_HEADS * MLA_V)
    return wq.astype(BF16), wk.astype(BF16), wv.astype(BF16)


def _gdn_gate_rows(a_log, dt_bias):
    zeros = jnp.zeros((GDN_HEADS,), F32)
    row = lambda v: _pad_cols(jnp.concatenate([v[0], zeros, v[1], zeros])[None, :], LANES)
    return row(a_log.astype(F32)), row(dt_bias.astype(F32))


def kernel(x, c, positions, ada_w, ada_b, norm_mix, norm_ffn, w_in, mla_q_norm, mla_w_uq, mla_kv_norm,
           mla_w_ukv, gdn_conv, gdn_a_log, gdn_dt_bias, gdn_out_norm, lru_conv_w, lru_conv_b, lru_w_a,
           lru_b_a, lru_w_i, lru_b_i, lru_lambda, w_branch_a, w_branch_b, w_branch_c, w_gate, w_out,
           ffn_w1, ffn_w3, ffn_w2, moe_router, moe_router_b, moe_w1, moe_w3, moe_w2, final_norm):
    bsz, seq, d = x.shape
    t = bsz * seq
    depth = ada_w.shape[0]
    xf = x.reshape(t, d)
    mod = ada_modulation(c, ada_w, ada_b).reshape(depth, bsz, 6, 1, d)
    cos, sin = rope_tables(positions)
    for layer in range(depth):
        shift_m, scale_m, gate_m, shift_f, scale_f, gate_f = (mod[layer, :, k] for k in range(6))
        p = norm_projection(xf, norm_mix[layer], scale_m, shift_m, projection_weight(w_in[layer], w_gate[layer]), seq)
        wq, wk, wv = mla_weights(mla_w_uq[layer], mla_w_ukv[layer])
        q = mla_q(p, mla_q_norm[layer], wq, cos, sin)
        k, v = mla_kv(p, mla_kv_norm[layer], wk, wv, cos, sin)
        o_a = flash_attention(q, k, v, bsz, seq)
        qkv = gdn_prep(p, gdn_conv[layer], seq)
        a_log_row, dt_row = _gdn_gate_rows(gdn_a_log[layer], gdn_dt_bias[layer])
        o_f, o_r = gdn_chunk_scan(qkv, p, a_log_row, dt_row, bsz, seq)
        o_b = gdn_output(o_f, o_r, p, gdn_out_norm[layer])
        o_c = rg_lru(p, lru_conv_w[layer], lru_conv_b[layer], lru_w_a[layer], lru_b_a[layer], lru_w_i[layer],
                     lru_b_i[layer], lru_lambda[layer], seq)
        merged = merge_branches(o_a, o_b, o_c, w_branch_a[layer].astype(BF16), w_branch_b[layer].astype(BF16),
                                w_branch_c[layer].astype(BF16), p)
        xf = matmul_gated_residual(merged, w_out[layer].astype(BF16), xf, gate_m, seq, tm=512, tn=d)
        j = layer // 2
        if layer % 2 == 0:
            act = norm_swiglu_up(xf, norm_ffn[layer], scale_f, shift_f, ffn_w1[j].astype(BF16),
                                 ffn_w3[j].astype(BF16), seq)
            xf = matmul_gated_residual(act, ffn_w2[j].astype(BF16), xf, gate_f, seq, tm=1024, tn=256)
        else:
            h, route = norm_router(xf, norm_ffn[layer], scale_f, shift_f, moe_router[j], moe_router_b[j], seq)
            row_tok, dest, block_e, n_used = routing_plan(route, t)
            act = expert_swiglu_up(block_e, n_used, expert_dispatch(n_used, row_tok, h), moe_w1[j], moe_w3[j])
            y = expert_down(block_e, n_used, act, moe_w2[j].astype(BF16))
            xf = combine_residual(xf, y, dest, route, gate_f, seq)
    return final_rms_norm(xf, final_norm).reshape(bsz, seq, d)
```

```python
import functools
import math

import jax
import jax.numpy as jnp
import numpy as np
from jax import lax
from jax.experimental import pallas as pl
from jax.experimental.pallas import tpu as pltpu

F32 = jnp.float32
BF16 = jnp.bfloat16
HIGHEST = lax.Precision.HIGHEST

LANES = 128
SUBLANES = 8
VMEM_LIMIT_BYTES = 56 * 1024 * 1024

NORM_EPS = 1e-6
ROPE_BASE = 10000.0
MLA_HEADS = 8
MLA_NOPE = 128
MLA_ROPE = 64
MLA_V = 128
MLA_QK_PAD = 256
GDN_HEADS = 8
GDN_DIM = 128
GDN_CHUNK = 64
LRU_BLOCKS = 8
LRU_C = 8.0
N_EXPERTS = 8
TOP_K = 2
MOE_BLOCK = 512
IN_FIXED_WIDTHS = (512, 256, MLA_ROPE, 3 * GDN_HEADS * GDN_DIM, GDN_HEADS * GDN_DIM, 4 * GDN_HEADS)
IN_FIXED_WIDTH = sum(IN_FIXED_WIDTHS)

P_GATE, P_QKV, P_Z, P_LRUX, P_LRUY, P_CQ, P_CKV, P_KROPE, P_AB, P_TOTAL = 0, 48, 72, 80, 88, 96, 100, 102, 103, 104


def _cparams(*semantics):
    return pltpu.CompilerParams(dimension_semantics=semantics, vmem_limit_bytes=VMEM_LIMIT_BYTES)


def _silu(x):
    return x * jax.nn.sigmoid(x)


def _softplus(x):
    return jnp.maximum(x, 0.0) + jnp.log1p(jnp.exp(-jnp.abs(x)))


def _bdot(a, b):
    return jnp.dot(a.astype(BF16), b.astype(BF16), preferred_element_type=F32)


def _bdot_nt(a, b):
    return lax.dot_general(a.astype(BF16), b.astype(BF16), (((1,), (1,)), ((), ())),
                           preferred_element_type=F32)


def _bdot_tn(a, b):
    return lax.dot_general(a.astype(BF16), b.astype(BF16), (((0,), (0,)), ((), ())),
                           preferred_element_type=F32)


def _fdot(a, b):
    return jnp.dot(a, b, preferred_element_type=F32, precision=HIGHEST)


def _ada_kernel(c_ref, w_ref, b_ref, o_ref):
    cond = _silu(c_ref[...])
    o_ref[...] = _bdot(cond, w_ref[...]) + b_ref[...]


def ada_modulation(c, ada_w, ada_b, *, tn=1024):
    n_layers, d, n = ada_w.shape
    bsz = c.shape[0]
    return pl.pallas_call(
        _ada_kernel,
        grid=(n_layers, n // tn),
        in_specs=[
            pl.BlockSpec((bsz, d), lambda l, j: (0, 0)),
            pl.BlockSpec((None, d, tn), lambda l, j: (l, 0, j)),
            pl.BlockSpec((None, 1, tn), lambda l, j: (l, 0, j)),
        ],
        out_specs=pl.BlockSpec((None, bsz, tn), lambda l, j: (l, 0, j)),
        out_shape=jax.ShapeDtypeStruct((n_layers, bsz, n), F32),
        compiler_params=_cparams("arbitrary", "arbitrary"),
        name="ada_modulation",
    )(c, ada_w, ada_b.reshape(n_layers, 1, n))


def _norm_mod(x, gain, scale, shift):
    y = x * lax.rsqrt(jnp.mean(x * x, axis=-1, keepdims=True) + NORM_EPS)
    return (y * gain) * (1.0 + scale) + shift


NORM_SLAB_ROWS = 256


def _store_norm_mod(x_ref, gain_ref, scale_ref, shift_ref, h_ref):
    gain, scale, shift = gain_ref[...], scale_ref[...], shift_ref[...]
    for r in range(0, x_ref.shape[0], NORM_SLAB_ROWS):
        rows = slice(r, r + NORM_SLAB_ROWS)
        h_ref[rows, :] = _norm_mod(x_ref[rows, :], gain, scale, shift).astype(BF16)


def _proj_kernel(x_ref, gain_ref, scale_ref, shift_ref, w_ref, o_ref, h_ref):
    @pl.when(pl.program_id(1) == 0)
    def _():
        _store_norm_mod(x_ref, gain_ref, scale_ref, shift_ref, h_ref)

    o_ref[...] = jnp.dot(h_ref[...], w_ref[...], preferred_element_type=F32)


def _row_specs(d, tm, tiles_per_batch):
    return [
        pl.BlockSpec((tm, d), lambda i, j: (i, 0)),
        pl.BlockSpec((1, d), lambda i, j: (0, 0)),
        pl.BlockSpec((None, 1, d), lambda i, j: (i // tiles_per_batch, 0, 0)),
        pl.BlockSpec((None, 1, d), lambda i, j: (i // tiles_per_batch, 0, 0)),
    ]


def norm_projection(x, gain, scale, shift, w, seq, *, tm=1024, tn=1024):
    t, d = x.shape
    n = w.shape[1]
    tm = min(tm, seq)
    return pl.pallas_call(
        _proj_kernel,
        grid=(t // tm, n // tn),
        in_specs=_row_specs(d, tm, seq // tm) + [pl.BlockSpec((d, tn), lambda i, j: (0, j))],
        out_specs=pl.BlockSpec((tm, tn), lambda i, j: (i, j)),
        out_shape=jax.ShapeDtypeStruct((t, n), F32),
        scratch_shapes=[pltpu.VMEM((tm, d), BF16)],
        compiler_params=_cparams("arbitrary", "arbitrary"),
        name="norm_projection",
    )(x, gain.reshape(1, d), scale, shift, w)


def _rope_kernel(pos_ref, freq_ref, cos_ref, sin_ref):
    ang = pos_ref[...] * freq_ref[...]
    live = lax.broadcasted_iota(jnp.int32, ang.shape, 1) < MLA_ROPE
    cos_ref[...] = jnp.where(live, jnp.cos(ang), 0.0)
    sin_ref[...] = jnp.where(live, jnp.sin(ang), 0.0)


def rope_tables(positions, *, tm=1024):
    t = positions.size
    inv_freq = ROPE_BASE ** (-np.arange(0, MLA_ROPE, 2, dtype=np.float32) / MLA_ROPE)
    freq = np.zeros((1, LANES), np.float32)
    freq[0, :MLA_ROPE] = np.tile(inv_freq, 2)
    pos = positions.astype(F32).reshape(t, 1)
    return pl.pallas_call(
        _rope_kernel,
        grid=(t // tm,),
        in_specs=[pl.BlockSpec((tm, 1), lambda i: (i, 0)), pl.BlockSpec((1, LANES), lambda i: (0, 0))],
        out_specs=[pl.BlockSpec((tm, LANES), lambda i: (i, 0))] * 2,
        out_shape=[jax.ShapeDtypeStruct((t, LANES), F32)] * 2,
        compiler_params=_cparams("arbitrary"),
        name="rope_tables",
    )(pos, jnp.asarray(freq))


def _rope_half(t, cos, sin):
    lane = lax.broadcasted_iota(jnp.int32, t.shape, 1)
    rot = jnp.where(lane < MLA_ROPE // 2, -pltpu.roll(t, LANES - MLA_ROPE // 2, 1), pltpu.roll(t, MLA_ROPE // 2, 1))
    return t * cos + rot * sin


def _rms(x, gain):
    return x * lax.rsqrt(jnp.mean(x * x, axis=-1, keepdims=True) + NORM_EPS) * gain


def _mla_q_kernel(cq_ref, gain_ref, w_ref, cos_ref, sin_ref, o_ref):
    scale = (MLA_NOPE + MLA_ROPE) ** -0.5 * math.log2(math.e)
    q = _bdot(_rms(cq_ref[...], gain_ref[...]), w_ref[...])
    cos, sin = cos_ref[...], sin_ref[...]
    for h in range(MLA_HEADS):
        lo = h * MLA_QK_PAD
        o_ref[:, lo:lo + LANES] = (q[:, lo:lo + LANES] * scale).astype(BF16)
        o_ref[:, lo + LANES:lo + 2 * LANES] = (_rope_half(q[:, lo + LANES:lo + 2 * LANES], cos, sin) * scale).astype(BF16)


def mla_q(p, gain, w_q, cos, sin, *, tm=512):
    t = p.shape[0]
    lora = gain.shape[0]
    n = w_q.shape[1]
    return pl.pallas_call(
        _mla_q_kernel,
        grid=(t // tm,),
        in_specs=[
            pl.BlockSpec((tm, lora), lambda i: (i, P_CQ * LANES // lora)),
            pl.BlockSpec((1, lora), lambda i: (0, 0)),
            pl.BlockSpec((lora, n), lambda i: (0, 0)),
            pl.BlockSpec((tm, LANES), lambda i: (i, 0)),
            pl.BlockSpec((tm, LANES), lambda i: (i, 0)),
        ],
        out_specs=pl.BlockSpec((tm, n), lambda i: (i, 0)),
        out_shape=jax.ShapeDtypeStruct((t, n), BF16),
        compiler_params=_cparams("arbitrary"),
        name="mla_q",
    )(p, gain.reshape(1, lora), w_q, cos, sin)


def _mla_kv_kernel(ckv_ref, kr_ref, gain_ref, wk_ref, wv_ref, cos_ref, sin_ref, k_ref, v_ref):
    n = _rms(ckv_ref[...], gain_ref[...]).astype(BF16)
    k_nope = jnp.dot(n, wk_ref[...], preferred_element_type=F32)
    v = jnp.dot(n, wv_ref[...], preferred_element_type=F32)
    k_rope = _rope_half(kr_ref[...], cos_ref[...], sin_ref[...]).astype(BF16)
    ones_lane = (lax.broadcasted_iota(jnp.int32, k_rope.shape, 1) == 0).astype(BF16)
    for h in range(MLA_HEADS):
        lo = h * MLA_QK_PAD
        k_ref[:, lo:lo + LANES] = k_nope[:, h * MLA_NOPE:(h + 1) * MLA_NOPE].astype(BF16)
        k_ref[:, lo + LANES:lo + 2 * LANES] = k_rope
        v_ref[:, lo:lo + LANES] = v[:, h * MLA_V:(h + 1) * MLA_V].astype(BF16)
        v_ref[:, lo + LANES:lo + 2 * LANES] = ones_lane


def mla_kv(p, gain, w_k, w_v, cos, sin, *, tm=512):
    t = p.shape[0]
    lora = gain.shape[0]
    return pl.pallas_call(
        _mla_kv_kernel,
        grid=(t // tm,),
        in_specs=[
            pl.BlockSpec((tm, lora), lambda i: (i, P_CKV * LANES // lora)),
            pl.BlockSpec((tm, LANES), lambda i: (i, P_KROPE)),
            pl.BlockSpec((1, lora), lambda i: (0, 0)),
            pl.BlockSpec(w_k.shape, lambda i: (0, 0)),
            pl.BlockSpec(w_v.shape, lambda i: (0, 0)),
            pl.BlockSpec((tm, LANES), lambda i: (i, 0)),
            pl.BlockSpec((tm, LANES), lambda i: (i, 0)),
        ],
        out_specs=[
            pl.BlockSpec((tm, MLA_HEADS * MLA_QK_PAD), lambda i: (i, 0)),
            pl.BlockSpec((tm, MLA_HEADS * MLA_QK_PAD), lambda i: (i, 0)),
        ],
        out_shape=[
            jax.ShapeDtypeStruct((t, MLA_HEADS * MLA_QK_PAD), BF16),
            jax.ShapeDtypeStruct((t, MLA_HEADS * MLA_QK_PAD), BF16),
        ],
        compiler_params=_cparams("arbitrary"),
        name="mla_kv",
    )(p, p, gain.reshape(1, lora), w_k, w_v, cos, sin)


def _flash_kernel(q_ref, k_ref, v_ref, o_ref, m_ref, acc_ref, *, tk):
    m_ref[...] = jnp.full(m_ref.shape, -jnp.inf, F32)
    acc_ref[...] = jnp.zeros(acc_ref.shape, F32)
    q = q_ref[...]

    def body(i, _):
        rows = pl.ds(pl.multiple_of(i * tk, tk), tk)
        s = lax.dot_general(q, k_ref[rows, :], (((1,), (1,)), ((), ())), preferred_element_type=F32)
        m_prev = m_ref[...]
        m_new = jnp.maximum(m_prev, jnp.max(s, axis=-1, keepdims=True))
        p = jnp.exp2(s - m_new).astype(BF16)
        acc_ref[...] = jnp.exp2(m_prev - m_new) * acc_ref[...] + jnp.dot(p, v_ref[rows, :],
                                                                         preferred_element_type=F32)
        m_ref[...] = m_new
        return 0

    lax.fori_loop(0, k_ref.shape[0] // tk, body, 0, unroll=True)
    acc = acc_ref[...]
    o_ref[...] = (acc[:, :MLA_V] / acc[:, MLA_V:MLA_V + 1]).astype(o_ref.dtype)


def flash_attention(q, k, v, bsz, seq, *, tq=512, tk=1024):
    t = q.shape[0]
    tq, tk = min(tq, seq), min(tk, seq)
    nq = seq // tq
    return pl.pallas_call(
        functools.partial(_flash_kernel, tk=tk),
        grid=(bsz, MLA_HEADS, nq),
        in_specs=[
            pl.BlockSpec((tq, MLA_QK_PAD), lambda b, h, qi: (b * nq + qi, h)),
            pl.BlockSpec((seq, MLA_QK_PAD), lambda b, h, qi: (b, h)),
            pl.BlockSpec((seq, MLA_QK_PAD), lambda b, h, qi: (b, h)),
        ],
        out_specs=pl.BlockSpec((tq, MLA_V), lambda b, h, qi: (b * nq + qi, h)),
        out_shape=jax.ShapeDtypeStruct((t, MLA_HEADS * MLA_V), BF16),
        scratch_shapes=[pltpu.VMEM((tq, 1), F32), pltpu.VMEM((tq, MLA_QK_PAD), F32)],
        compiler_params=_cparams("arbitrary", "arbitrary", "arbitrary"),
        name="flash_attention",
    )(q, k, v)


def _conv4(main_ref, prev_ref, next_ref, w_ref, pad_ref, first, last):
    tm = main_ref.shape[0]
    x = main_ref[...]
    pad_ref[0:SUBLANES, :] = jnp.where(first, 0.0, prev_ref[...])
    pad_ref[SUBLANES:SUBLANES + tm, :] = x
    pad_ref[SUBLANES + tm:2 * SUBLANES + tm, :] = jnp.where(last, 0.0, next_ref[...])
    w = w_ref[...]
    return (w[0:1] * pad_ref[SUBLANES - 1:SUBLANES - 1 + tm, :] + w[1:2] * x
            + w[2:3] * pad_ref[SUBLANES + 1:SUBLANES + 1 + tm, :]
            + w[3:4] * pad_ref[SUBLANES + 2:SUBLANES + 2 + tm, :])


def _halo_specs(tm, width, col, n_row8, grid_rank):
    r8 = tm // SUBLANES
    if grid_rank == 1:
        return [
            pl.BlockSpec((tm, width), lambda i: (i, col)),
            pl.BlockSpec((SUBLANES, width), lambda i: (jnp.maximum(i * r8 - 1, 0), col)),
            pl.BlockSpec((SUBLANES, width), lambda i: (jnp.minimum((i + 1) * r8, n_row8 - 1), col)),
        ]
    return [
        pl.BlockSpec((tm, width), lambda i, j: (i, col + j)),
        pl.BlockSpec((SUBLANES, width), lambda i, j: (jnp.maximum(i * r8 - 1, 0), col + j)),
        pl.BlockSpec((SUBLANES, width), lambda i, j: (jnp.minimum((i + 1) * r8, n_row8 - 1), col + j)),
    ]


def _gdn_prep_kernel(main_ref, prev_ref, next_ref, w_ref, o_ref, pad_ref, *, tiles_per_batch):
    i, j = pl.program_id(0), pl.program_id(1)
    first = (i % tiles_per_batch) == 0
    last = (i % tiles_per_batch) == tiles_per_batch - 1
    y = _silu(_conv4(main_ref, prev_ref, next_ref, w_ref, pad_ref, first, last))

    @pl.when(j < 2)
    def _():
        post = jnp.where(j == 0, GDN_DIM ** -0.5, 1.0)
        for h in range(GDN_HEADS):
            seg = y[:, h * GDN_DIM:(h + 1) * GDN_DIM]
            inv = lax.rsqrt(jnp.sum(seg * seg, axis=-1, keepdims=True) + NORM_EPS)
            o_ref[:, h * GDN_DIM:(h + 1) * GDN_DIM] = seg * (inv * post)

    @pl.when(j == 2)
    def _():
        o_ref[...] = y


def gdn_prep(p, conv_w, seq, *, tm=512):
    t = p.shape[0]
    width = GDN_HEADS * GDN_DIM
    col = P_QKV * LANES // width
    return pl.pallas_call(
        functools.partial(_gdn_prep_kernel, tiles_per_batch=seq // tm),
        grid=(t // tm, 3),
        in_specs=_halo_specs(tm, width, col, t // SUBLANES, 2) + [pl.BlockSpec((4, width), lambda i, j: (0, j))],
        out_specs=pl.BlockSpec((tm, width), lambda i, j: (i, j)),
        out_shape=jax.ShapeDtypeStruct((t, 3 * width), F32),
        scratch_shapes=[pltpu.VMEM((tm + 2 * SUBLANES, width), F32)],
        compiler_params=_cparams("arbitrary", "arbitrary"),
        name="gdn_prep",
    )(p, p, p, conv_w)


def _unit_triangular_inverse(low, dot):
    c = low[0].shape[0]
    heads = range(len(low))
    row = lax.broadcasted_iota(jnp.int32, (c, c), 0)
    col = lax.broadcasted_iota(jnp.int32, (c, c), 1)
    eye = (row == col).astype(F32)
    inv = [eye - jnp.where((row ^ 1) == col, low[h], 0.0) for h in heads]
    shift = 1
    while (2 << shift) <= c:
        off_block = ((row >> shift) ^ 1) == (col >> shift)
        blk = [jnp.where(off_block, low[h], 0.0) for h in heads]
        half = [dot(inv[h], blk[h]) for h in heads]
        inv = [inv[h] - dot(half[h], inv[h]) for h in heads]
        shift += 1
    return inv


def _gdn_chunk_kernel(qf_ref, kf_ref, vf_ref, abf_ref, qr_ref, kr_ref, vr_ref, abr_ref, alog_ref, dt_ref,
                      of_ref, or_ref, state_ref):
    c = qf_ref.shape[0]

    @pl.when(pl.program_id(1) == 0)
    def _():
        state_ref[...] = jnp.zeros(state_ref.shape, F32)

    row = lax.broadcasted_iota(jnp.int32, (c, c), 0)
    col = lax.broadcasted_iota(jnp.int32, (c, c), 1)
    eye_l = (lax.broadcasted_iota(jnp.int32, (LANES, LANES), 0)
             == lax.broadcasted_iota(jnp.int32, (LANES, LANES), 1)).astype(F32)
    sl = [slice(h * GDN_DIM, (h + 1) * GDN_DIM) for h in range(GDN_HEADS)]

    q, k, v, g_col, g_row, beta, incl, strict, g_last = [], [], [], [], [], [], [], [], []
    for reverse, (q_ref, k_ref, v_ref, ab_ref) in enumerate(((qf_ref, kf_ref, vf_ref, abf_ref),
                                                              (qr_ref, kr_ref, vr_ref, abr_ref))):
        ab = ab_ref[...]
        log_alpha = -jnp.exp(alog_ref[...]) * _softplus(ab + dt_ref[...])
        beta_all = jax.nn.sigmoid(ab)
        cum = ((col >= row) if reverse else (col <= row)).astype(F32)
        g_all = _fdot(cum, log_alpha)
        g_rows = lax.dot_general(eye_l, g_all, (((1,), (1,)), ((), ())), preferred_element_type=F32,
                                 precision=HIGHEST)
        a_lane = 2 * GDN_HEADS * reverse
        b_lane = a_lane + GDN_HEADS
        last = 0 if reverse else c - 1
        for h in range(GDN_HEADS):
            q.append(q_ref[:, sl[h]])
            k.append(k_ref[:, sl[h]])
            v.append(v_ref[:, sl[h]])
            g_col.append(g_all[:, a_lane + h:a_lane + h + 1])
            g_row.append(g_rows[a_lane + h:a_lane + h + 1, :])
            beta.append(beta_all[:, b_lane + h:b_lane + h + 1])
            incl.append((row <= col) if reverse else (row >= col))
            strict.append((row < col) if reverse else (row > col))
            g_last.append(g_all[last:last + 1, a_lane + h:a_lane + h + 1])

    chains = range(2 * GDN_HEADS)
    state = [state_ref[n] for n in chains]
    decay = [jnp.exp(jnp.where(incl[n], g_col[n] - g_row[n], -jnp.inf)) for n in chains]
    eg = [jnp.exp(g_col[n]) for n in chains]
    kb = [k[n] * beta[n] for n in chains]
    low = [jnp.where(strict[n], _bdot_nt(kb[n], k[n]) * decay[n], 0.0) for n in chains]
    qk = [_bdot_nt(q[n], k[n]) * decay[n] for n in chains]
    inv = _unit_triangular_inverse(low, _bdot)
    sol = [_bdot(inv[n], jnp.concatenate([v[n] * beta[n], kb[n] * eg[n]], axis=1)) for n in chains]
    v_new = [sol[n][:, :GDN_DIM] - _bdot(sol[n][:, GDN_DIM:], state[n]) for n in chains]
    out = [_bdot(q[n] * eg[n], state[n]) + _bdot(qk[n], v_new[n]) for n in chains]
    new_state = [state[n] * jnp.exp(g_last[n]) + _bdot_tn(k[n] * jnp.exp(g_last[n] - g_col[n]), v_new[n])
                 for n in chains]
    for h in range(GDN_HEADS):
        of_ref[:, sl[h]] = out[h]
        or_ref[:, sl[h]] = out[GDN_HEADS + h]
    for n in chains:
        state_ref[n] = new_state[n]


def gdn_chunk_scan(qkv, p, a_log_row, dt_row, bsz, seq):
    t = qkv.shape[0]
    c = GDN_CHUNK
    n = seq // c
    width = GDN_HEADS * GDN_DIM
    fwd = lambda b, i: b * n + i
    rev = lambda b, i: b * n + n - 1 - i
    chunk = lambda rows, col: pl.BlockSpec((c, width), lambda b, i: (rows(b, i), col))
    gates = lambda rows: pl.BlockSpec((c, LANES), lambda b, i: (rows(b, i), P_AB))
    param = pl.BlockSpec((1, LANES), lambda b, i: (0, 0))
    return pl.pallas_call(
        _gdn_chunk_kernel,
        grid=(bsz, n),
        in_specs=[chunk(fwd, 0), chunk(fwd, 1), chunk(fwd, 2), gates(fwd),
                  chunk(rev, 0), chunk(rev, 1), chunk(rev, 2), gates(rev), param, param],
        out_specs=[chunk(fwd, 0), chunk(rev, 0)],
        out_shape=[jax.ShapeDtypeStruct((t, width), F32)] * 2,
        scratch_shapes=[pltpu.VMEM((2 * GDN_HEADS, GDN_DIM, GDN_DIM), F32)],
        compiler_params=_cparams("arbitrary", "arbitrary"),
        name="gdn_chunk_scan",
    )(qkv, qkv, qkv, p, qkv, qkv, qkv, p, a_log_row, dt_row)


def _gdn_out_kernel(of_ref, ob_ref, z_ref, gain_ref, o_ref):
    o = of_ref[...] + ob_ref[...]
    z = z_ref[...]
    gain = gain_ref[...]
    for h in range(GDN_HEADS):
        sl = slice(h * GDN_DIM, (h + 1) * GDN_DIM)
        o_ref[:, sl] = (_rms(o[:, sl], gain) * _silu(z[:, sl])).astype(BF16)


def gdn_output(o_f, o_b, p, out_norm, *, tm=512):
    t, width = o_f.shape
    return pl.pallas_call(
        _gdn_out_kernel,
        grid=(t // tm,),
        in_specs=[
            pl.BlockSpec((tm, width), lambda i: (i, 0)),
            pl.BlockSpec((tm, width), lambda i: (i, 0)),
            pl.BlockSpec((tm, width), lambda i: (i, P_Z * LANES // width)),
            pl.BlockSpec((1, GDN_DIM), lambda i: (0, 0)),
        ],
        out_specs=pl.BlockSpec((tm, width), lambda i: (i, 0)),
        out_shape=jax.ShapeDtypeStruct((t, width), BF16),
        compiler_params=_cparams("arbitrary"),
        name="gdn_output",
    )(o_f, o_b, p, out_norm.reshape(1, GDN_DIM))


def _lru_scan_tile(a, u, carry_ref, a_ref, b_ref, h_ref, reverse):
    tm = a.shape[0]
    sub = lax.broadcasted_iota(jnp.int32, a.shape, 0) % SUBLANES
    shift = 1
    while shift < SUBLANES:
        if reverse:
            ok = sub < SUBLANES - shift
            a_sh, u_sh = pltpu.roll(a, tm - shift, 0), pltpu.roll(u, tm - shift, 0)
        else:
            ok = sub >= shift
            a_sh, u_sh = pltpu.roll(a, shift, 0), pltpu.roll(u, shift, 0)
        u = a * jnp.where(ok, u_sh, 0.0) + u
        a = a * jnp.where(ok, a_sh, 1.0)
        shift *= 2
    a_ref[...] = a
    b_ref[...] = u
    groups = tm // SUBLANES
    edge = 0 if reverse else SUBLANES - 1

    def body(g, h_in):
        r0 = pl.multiple_of((groups - 1 - g if reverse else g) * SUBLANES, SUBLANES)
        h = a_ref[pl.ds(r0, SUBLANES), :] * h_in + b_ref[pl.ds(r0, SUBLANES), :]
        h_ref[pl.ds(r0, SUBLANES), :] = h
        return jnp.broadcast_to(h[edge:edge + 1, :], h.shape)

    carry_ref[...] = lax.fori_loop(0, groups, body, carry_ref[...], unroll=4)


def _lru_gates(xc, w_ref, ba_ref, bi_ref, lam_ref):
    blk = xc.shape[1] // LRU_BLOCKS
    parts = [_bdot(xc[:, n * blk:(n + 1) * blk], w_ref[n]) for n in range(LRU_BLOCKS)]
    r = jax.nn.sigmoid(jnp.concatenate([p[:, :blk] for p in parts], axis=1) + ba_ref[...])
    gate_i = jax.nn.sigmoid(jnp.concatenate([p[:, blk:] for p in parts], axis=1) + bi_ref[...])
    log_a = -LRU_C * r * _softplus(-lam_ref[...])
    a = jnp.exp(log_a)
    return a, xc * gate_i * jnp.sqrt(1.0 - a * a)


def _lru_fwd_kernel(main_ref, prev_ref, next_ref, cw_ref, cb_ref, w_ref, ba_ref, bi_ref, lam_ref,
                    h_ref, pad_ref, carry_ref, a_ref, b_ref, *, tiles_per_batch):
    i = pl.program_id(0) % tiles_per_batch

    @pl.when(i == 0)
    def _():
        carry_ref[...] = jnp.zeros(carry_ref.shape, F32)

    xc = _conv4(main_ref, prev_ref, next_ref, cw_ref, pad_ref, i == 0, i == tiles_per_batch - 1) + cb_ref[...]
    a, u = _lru_gates(xc, w_ref, ba_ref, bi_ref, lam_ref)
    _lru_scan_tile(a, u, carry_ref, a_ref, b_ref, h_ref, False)


def _lru_bwd_kernel(main_ref, prev_ref, next_ref, cw_ref, cb_ref, w_ref, ba_ref, bi_ref, lam_ref, hf_ref, y_ref,
                    o_ref, pad_ref, carry_ref, a_ref, b_ref, h_ref, *, tiles_per_batch):
    i = pl.program_id(0) % tiles_per_batch

    @pl.when(i == 0)
    def _():
        carry_ref[...] = jnp.zeros(carry_ref.shape, F32)

    xc = _conv4(main_ref, prev_ref, next_ref, cw_ref, pad_ref, i == tiles_per_batch - 1, i == 0) + cb_ref[...]
    a, u = _lru_gates(xc, w_ref, ba_ref, bi_ref, lam_ref)
    _lru_scan_tile(a, u, carry_ref, a_ref, b_ref, h_ref, True)
    o_ref[...] = ((hf_ref[...] + h_ref[...]) * jax.nn.gelu(y_ref[...], approximate=True)).astype(BF16)


def _lru_param_specs(width, blk):
    return [
        pl.BlockSpec((4, width), lambda i: (0, 0)),
        pl.BlockSpec((1, width), lambda i: (0, 0)),
        pl.BlockSpec((LRU_BLOCKS, blk, 2 * blk), lambda i: (0, 0, 0)),
        pl.BlockSpec((1, width), lambda i: (0, 0)),
        pl.BlockSpec((1, width), lambda i: (0, 0)),
        pl.BlockSpec((1, width), lambda i: (0, 0)),
    ]


def _lru_scratch(tm, width):
    return [pltpu.VMEM((tm + 2 * SUBLANES, width), F32), pltpu.VMEM((SUBLANES, width), F32),
            pltpu.VMEM((tm, width), F32), pltpu.VMEM((tm, width), F32)]


def rg_lru(p, conv_w, conv_b, w_a, b_a, w_i, b_i, lam, seq, *, tm=256):
    t = p.shape[0]
    width = conv_w.shape[1]
    blk = width // LRU_BLOCKS
    n_tiles = t // tm
    col_x, col_y = P_LRUX * LANES // width, P_LRUY * LANES // width
    w_dir = [jnp.concatenate([w_a[d], w_i[d]], axis=-1).astype(BF16) for d in range(2)]
    params = lambda d: (conv_w, conv_b.reshape(1, width), w_dir[d], b_a[d].reshape(1, width),
                        b_i[d].reshape(1, width), lam[d].reshape(1, width))
    h_f = pl.pallas_call(
        functools.partial(_lru_fwd_kernel, tiles_per_batch=seq // tm),
        grid=(n_tiles,),
        in_specs=_halo_specs(tm, width, col_x, t // SUBLANES, 1) + _lru_param_specs(width, blk),
        out_specs=pl.BlockSpec((tm, width), lambda i: (i, 0)),
        out_shape=jax.ShapeDtypeStruct((t, width), F32),
        scratch_shapes=_lru_scratch(tm, width),
        compiler_params=_cparams("arbitrary"),
        name="rg_lru_fwd",
    )(p, p, p, *params(0))
    r8 = tm // SUBLANES
    n_row8 = t // SUBLANES
    rev = lambda i: n_tiles - 1 - i
    halo_rev = [
        pl.BlockSpec((tm, width), lambda i: (rev(i), col_x)),
        pl.BlockSpec((SUBLANES, width), lambda i: (jnp.maximum(rev(i) * r8 - 1, 0), col_x)),
        pl.BlockSpec((SUBLANES, width), lambda i: (jnp.minimum((rev(i) + 1) * r8, n_row8 - 1), col_x)),
    ]
    return pl.pallas_call(
        functools.partial(_lru_bwd_kernel, tiles_per_batch=seq // tm),
        grid=(n_tiles,),
        in_specs=halo_rev + _lru_param_specs(width, blk) + [
            pl.BlockSpec((tm, width), lambda i: (rev(i), 0)),
            pl.BlockSpec((tm, width), lambda i: (rev(i), col_y)),
        ],
        out_specs=pl.BlockSpec((tm, width), lambda i: (rev(i), 0)),
        out_shape=jax.ShapeDtypeStruct((t, width), BF16),
        scratch_shapes=_lru_scratch(tm, width) + [pltpu.VMEM((tm, width), F32)],
        compiler_params=_cparams("arbitrary"),
        name="rg_lru_bwd",
    )(p, p, p, *params(1), h_f, p)


def _merge_kernel(oa_ref, ob_ref, oc_ref, wa_ref, wb_ref, wc_ref, ga_ref, gb_ref, gc_ref, o_ref):
    acc = jax.nn.sigmoid(ga_ref[...]) * jnp.dot(oa_ref[...], wa_ref[...], preferred_element_type=F32)
    acc += jax.nn.sigmoid(gb_ref[...]) * jnp.dot(ob_ref[...], wb_ref[...], preferred_element_type=F32)
    acc += jax.nn.sigmoid(gc_ref[...]) * jnp.dot(oc_ref[...], wc_ref[...], preferred_element_type=F32)
    o_ref[...] = acc.astype(BF16)


def merge_branches(o_a, o_b, o_c, w_a, w_b, w_c, p, *, tm=256, tn=2048):
    t = o_a.shape[0]
    d = w_a.shape[1]
    n_col = d // tn
    branch = lambda arr: pl.BlockSpec((tm, arr.shape[1]), lambda j, i: (i, 0))
    weight = lambda arr: pl.BlockSpec((arr.shape[0], tn), lambda j, i: (0, j), pipeline_mode=pl.Buffered(1))
    gate = lambda g: pl.BlockSpec((tm, tn), lambda j, i: (i, P_GATE * LANES // tn + g * n_col + j))
    return pl.pallas_call(
        _merge_kernel,
        grid=(n_col, t // tm),
        in_specs=[branch(o_a), branch(o_b), branch(o_c), weight(w_a), weight(w_b), weight(w_c),
                  gate(0), gate(1), gate(2)],
        out_specs=pl.BlockSpec((tm, tn), lambda j, i: (i, j)),
        out_shape=jax.ShapeDtypeStruct((t, d), BF16),
        compiler_params=_cparams("arbitrary", "arbitrary"),
        name="merge_branches",
    )(o_a, o_b, o_c, w_a, w_b, w_c, p, p, p)


def _residual_kernel(a_ref, w_ref, x_ref, g_ref, o_ref):
    o_ref[...] = x_ref[...] + g_ref[...] * jnp.dot(a_ref[...], w_ref[...], preferred_element_type=F32)


def matmul_gated_residual(a, w, x, gate, seq, *, tm=512, tn=512):
    t, k = a.shape
    d = w.shape[1]
    tm = min(tm, seq)
    tiles_per_batch = seq // tm
    return pl.pallas_call(
        _residual_kernel,
        grid=(t // tm, d // tn),
        in_specs=[
            pl.BlockSpec((tm, k), lambda i, j: (i, 0)),
            pl.BlockSpec((k, tn), lambda i, j: (0, j)),
            pl.BlockSpec((tm, tn), lambda i, j: (i, j)),
            pl.BlockSpec((None, 1, tn), lambda i, j: (i // tiles_per_batch, 0, j)),
        ],
        out_specs=pl.BlockSpec((tm, tn), lambda i, j: (i, j)),
        out_shape=jax.ShapeDtypeStruct((t, d), F32),
        compiler_params=_cparams("arbitrary", "arbitrary"),
        name="matmul_gated_residual",
    )(a, w, x, gate)


def _swiglu_up_kernel(x_ref, gain_ref, scale_ref, shift_ref, w1_ref, w3_ref, o_ref, h_ref):
    @pl.when(pl.program_id(1) == 0)
    def _():
        _store_norm_mod(x_ref, gain_ref, scale_ref, shift_ref, h_ref)

    h = h_ref[...]
    up = jnp.dot(h, w1_ref[...], preferred_element_type=F32)
    o_ref[...] = (_silu(up) * jnp.dot(h, w3_ref[...], preferred_element_type=F32)).astype(BF16)


def norm_swiglu_up(x, gain, scale, shift, w1, w3, seq, *, tm=1024, tn=512):
    t, d = x.shape
    f = w1.shape[1]
    tm = min(tm, seq)
    return pl.pallas_call(
        _swiglu_up_kernel,
        grid=(t // tm, f // tn),
        in_specs=_row_specs(d, tm, seq // tm) + [pl.BlockSpec((d, tn), lambda i, j: (0, j))] * 2,
        out_specs=pl.BlockSpec((tm, tn), lambda i, j: (i, j)),
        out_shape=jax.ShapeDtypeStruct((t, f), BF16),
        scratch_shapes=[pltpu.VMEM((tm, d), BF16)],
        compiler_params=_cparams("arbitrary", "arbitrary"),
        name="norm_swiglu_up",
    )(x, gain.reshape(1, d), scale, shift, w1, w3)


def _router_kernel(x_ref, gain_ref, scale_ref, shift_ref, w_ref, b_ref, h_ref, route_ref):
    h = _norm_mod(x_ref[...], gain_ref[...], scale_ref[...], shift_ref[...])
    h_ref[...] = h
    lane =lax.broadcasted_iota(jnp.int32, route_ref.shape, 1)
    logits = jnp.where(lane < N_EXPERTS, _fdot(h, w_ref[...]) + b_ref[...], -jnp.inf)
    top1 = jnp.max(logits, axis=-1, keepdims=True)
    idx1 = jnp.min(jnp.where(logits == top1, lane, LANES), axis=-1, keepdims=True)
    rest = jnp.where(lane == idx1, -jnp.inf, logits)
    top2 = jnp.max(rest, axis=-1, keepdims=True)
    idx2 = jnp.min(jnp.where(rest == top2, lane, LANES), axis=-1, keepdims=True)
    e2 = jnp.exp(top2 - top1)
    inv = 1.0 / (1.0 + e2)
    route_ref[...] = jnp.where(lane == 0, idx1.astype(F32),
                               jnp.where(lane == 1, idx2.astype(F32),
                                         jnp.where(lane == 2, inv, jnp.where(lane == 3, e2 * inv, 0.0))))


def norm_router(x, gain, scale, shift, w_router, b_router, seq, *, tm=512):
    t, d = x.shape
    w_pad = jnp.zeros((d, LANES), F32).at[:, :N_EXPERTS].set(w_router)
    b_pad = jnp.zeros((1, LANES), F32).at[0, :N_EXPERTS].set(b_router)
    tiles_per_batch = seq // tm
    return pl.pallas_call(
        _router_kernel,
        grid=(t // tm,),
        in_specs=[
            pl.BlockSpec((tm, d), lambda i: (i, 0)),
            pl.BlockSpec((1, d), lambda i: (0, 0)),
            pl.BlockSpec((None, 1, d), lambda i: (i // tiles_per_batch, 0, 0)),
            pl.BlockSpec((None, 1, d), lambda i: (i // tiles_per_batch, 0, 0)),
            pl.BlockSpec((d, LANES), lambda i: (0, 0)),
            pl.BlockSpec((1, LANES), lambda i: (0, 0)),
        ],
        out_specs=[pl.BlockSpec((tm, d), lambda i: (i, 0)), pl.BlockSpec((tm, LANES), lambda i: (i, 0))],
        out_shape=[jax.ShapeDtypeStruct((t, d), F32), jax.ShapeDtypeStruct((t, LANES), F32)],
        compiler_params=_cparams("arbitrary"),
        name="norm_router",
    )(x, gain.reshape(1, d), scale, shift, w_pad, b_pad)


def _row_copy(src_hbm, buf_ref, slot, r, src_row, sem_rows):
    return pltpu.make_async_copy(src_hbm.at[pl.ds(src_row, 1)], buf_ref.at[slot, pl.ds(r, 1)], sem_rows.at[slot])


def _start_row_gather(src_hbm, jobs, slot, sem_idx, sem_rows):
    for idx_ref, idx_smem, buf_ref in jobs:
        load = pltpu.make_async_copy(idx_ref, idx_smem.at[slot], sem_idx)
        load.start()
        load.wait()

        def issue(r, carry, idx_smem=idx_smem, buf_ref=buf_ref):
            _row_copy(src_hbm, buf_ref, slot, r, idx_smem[slot, 0, 0, r], sem_rows).start()
            return carry

        lax.fori_loop(0, buf_ref.shape[1], issue, 0, unroll=8)


def _wait_row_gather(src_hbm, jobs, slot, sem_rows):
    for _, _, buf_ref in jobs:
        def drain(r, carry, buf_ref=buf_ref):
            _row_copy(src_hbm, buf_ref, slot, r, 0, sem_rows).wait()
            return carry

        lax.fori_loop(0, buf_ref.shape[1], drain, 0, unroll=8)


def _pipelined_row_gather(step, n_live, src_hbm, jobs_now, jobs_next, sem_idx, sem_rows):
    slot = step % 2

    @pl.when(step == 0)
    def _():
        _start_row_gather(src_hbm, jobs_now, 0, sem_idx, sem_rows)

    @pl.when(step + 1 < n_live)
    def _():
        _start_row_gather(src_hbm, jobs_next, 1 - slot, sem_idx, sem_rows)

    _wait_row_gather(src_hbm, jobs_now, slot, sem_rows)
    return slot


def _gather_scratch(rows, d, n_jobs):
    return ([pltpu.VMEM((2, rows, d), F32)] * n_jobs + [pltpu.SMEM((2, 1, 1, rows), jnp.int32)] * n_jobs
            + [pltpu.SemaphoreType.DMA(()), pltpu.SemaphoreType.DMA((2,))])


def _dispatch_kernel(used_ref, idx_ref, idx_next_ref, h_hbm, o_ref, buf_ref, idx_smem, sem_idx, sem_rows):
    step = pl.program_id(0)

    @pl.when(step < used_ref[0])
    def _():
        slot = _pipelined_row_gather(step, used_ref[0], h_hbm, [(idx_ref, idx_smem, buf_ref)],
                                     [(idx_next_ref, idx_smem, buf_ref)], sem_idx, sem_rows)
        o_ref[...] = buf_ref[slot].astype(BF16)

    @pl.when(step >= used_ref[0])
    def _():
        o_ref[...] = jnp.zeros(o_ref.shape, BF16)


def expert_dispatch(n_used, row_tok, h):
    d = h.shape[1]
    n_blocks = row_tok.shape[0] // MOE_BLOCK
    table = row_tok.reshape(n_blocks, 1, MOE_BLOCK)
    return pl.pallas_call(
        _dispatch_kernel,
        grid_spec=pltpu.PrefetchScalarGridSpec(
            num_scalar_prefetch=1,
            grid=(n_blocks,),
            in_specs=[
                pl.BlockSpec((1, 1, MOE_BLOCK), lambda i, used: (i, 0, 0)),
                pl.BlockSpec((1, 1, MOE_BLOCK), lambda i, used: (jnp.minimum(i + 1, n_blocks - 1), 0, 0)),
                pl.BlockSpec(memory_space=pl.ANY),
            ],
            out_specs=pl.BlockSpec((MOE_BLOCK, d), lambda i, used: (i, 0)),
            scratch_shapes=_gather_scratch(MOE_BLOCK, d, 1),
        ),
        out_shape=jax.ShapeDtypeStruct((n_blocks * MOE_BLOCK, d), BF16),
        compiler_params=_cparams("arbitrary"),
        name="expert_dispatch",
    )(n_used, table, table, h)


def _expert_up_kernel(be_ref, used_ref, x_ref, w1_ref, w3_ref, o_ref, w1b_ref, w3b_ref):
    i = pl.program_id(1)

    @pl.when(jnp.logical_or(i == 0, be_ref[i] != be_ref[jnp.maximum(i - 1, 0)]))
    def _():
        w1b_ref[...] = w1_ref[...].astype(BF16)
        w3b_ref[...] = w3_ref[...].astype(BF16)

    @pl.when(i < used_ref[0])
    def _():
        x = x_ref[...]
        up = jnp.dot(x, w1b_ref[...], preferred_element_type=F32)
        o_ref[...] = (_silu(up) * jnp.dot(x, w3b_ref[...], preferred_element_type=F32)).astype(BF16)

    @pl.when(i >= used_ref[0])
    def _():
        o_ref[...] = jnp.zeros(o_ref.shape, BF16)


def expert_swiglu_up(block_e, n_used, xs, w1, w3, *, tn=512):
    rows, d = xs.shape
    f = w1.shape[2]
    return pl.pallas_call(
        _expert_up_kernel,
        grid_spec=pltpu.PrefetchScalarGridSpec(
            num_scalar_prefetch=2,
            grid=(f // tn, rows // MOE_BLOCK),
            in_specs=[
                pl.BlockSpec((MOE_BLOCK, d), lambda j, i, be, used: (i, 0)),
                pl.BlockSpec((None, d, tn), lambda j, i, be, used: (be[i], 0, j)),
                pl.BlockSpec((None, d, tn), lambda j, i, be, used: (be[i], 0, j)),
            ],
            out_specs=pl.BlockSpec((MOE_BLOCK, tn), lambda j, i, be, used: (i, j)),
            scratch_shapes=[pltpu.VMEM((d, tn), BF16), pltpu.VMEM((d, tn), BF16)],
        ),
        out_shape=jax.ShapeDtypeStruct((rows, f), BF16),
        compiler_params=_cparams("arbitrary", "arbitrary"),
        name="expert_swiglu_up",
    )(block_e, n_used, xs, w1, w3)


def _expert_down_kernel(be_ref, used_ref, a_ref, w_ref, o_ref):
    i = pl.program_id(1)

    @pl.when(i < used_ref[0])
    def _():
        o_ref[...] = jnp.dot(a_ref[...], w_ref[...], preferred_element_type=F32)

    @pl.when(i >= used_ref[0])
    def _():
        o_ref[...] = jnp.zeros(o_ref.shape, F32)


def expert_down(block_e, n_used, act, w2, *, tn=1024):
    rows, f = act.shape
    d = w2.shape[2]
    return pl.pallas_call(
        _expert_down_kernel,
        grid_spec=pltpu.PrefetchScalarGridSpec(
            num_scalar_prefetch=2,
            grid=(d // tn, rows // MOE_BLOCK),
            in_specs=[
                pl.BlockSpec((MOE_BLOCK, f), lambda j, i, be, used: (i, 0)),
                pl.BlockSpec((None, f, tn), lambda j, i, be, used: (be[i], 0, j)),
            ],
            out_specs=pl.BlockSpec((MOE_BLOCK, tn), lambda j, i, be, used: (i, j)),
        ),
        out_shape=jax.ShapeDtypeStruct((rows, d), F32),
        compiler_params=_cparams("arbitrary", "arbitrary"),
        name="expert_down",
    )(block_e, n_used, act, w2)


def _combine_kernel(idx1_ref, idx2_ref, idx1_next_ref, idx2_next_ref, x_ref, route_ref, g_ref, y_hbm, o_ref,
                    buf1_ref, buf2_ref, idx1_smem, idx2_smem, sem_idx, sem_rows):
    slot = _pipelined_row_gather(
        pl.program_id(0), pl.num_programs(0), y_hbm,
        [(idx1_ref, idx1_smem, buf1_ref), (idx2_ref, idx2_smem, buf2_ref)],
        [(idx1_next_ref, idx1_smem, buf1_ref), (idx2_next_ref, idx2_smem, buf2_ref)], sem_idx, sem_rows)
    route = route_ref[...]
    y = route[:, 2:3] * buf1_ref[slot] + route[:, 3:4] * buf2_ref[slot]
    o_ref[...] = x_ref[...] + g_ref[...] * y


def combine_residual(x, y, dest, route, gate, seq, *, tm=256):
    t, d = x.shape
    tiles_per_batch = seq // tm
    n_tiles = t // tm
    row = pl.BlockSpec((tm, d), lambda i: (i, 0))
    idx = pl.BlockSpec((1, 1, tm), lambda i: (i, 0, 0))
    idx_next = pl.BlockSpec((1, 1, tm), lambda i: (jnp.minimum(i + 1, n_tiles - 1), 0, 0))
    tables = [dest[:, k].reshape(n_tiles, 1, tm) for k in range(TOP_K)]
    return pl.pallas_call(
        _combine_kernel,
        grid=(n_tiles,),
        in_specs=[idx, idx, idx_next, idx_next, row, pl.BlockSpec((tm, LANES), lambda i: (i, 0)),
                  pl.BlockSpec((None, 1, d), lambda i: (i // tiles_per_batch, 0, 0)),
                  pl.BlockSpec(memory_space=pl.ANY)],
        out_specs=row,
        out_shape=jax.ShapeDtypeStruct((t, d), F32),
        scratch_shapes=_gather_scratch(tm, d, TOP_K),
        compiler_params=_cparams("arbitrary"),
        name="combine_residual",
    )(*tables, *tables, x, route, gate, y)


def routing_plan(route, n_tokens):
    n_pairs = n_tokens * TOP_K
    flat_e = route[:, :TOP_K].astype(jnp.int32).reshape(n_pairs)
    onehot = (flat_e[:, None] == jnp.arange(N_EXPERTS, dtype=jnp.int32)[None, :]).astype(jnp.int32)
    running = jnp.cumsum(onehot, axis=0)
    counts = running[-1]
    rank = jnp.take_along_axis(running, flat_e[:, None], axis=1)[:, 0] - 1
    padded = (counts + MOE_BLOCK - 1) // MOE_BLOCK * MOE_BLOCK
    pend = jnp.cumsum(padded)
    dest = (pend - padded)[flat_e] + rank
    n_blocks = -(-n_pairs // MOE_BLOCK) + N_EXPERTS
    flat_tok = jnp.repeat(jnp.arange(n_tokens, dtype=jnp.int32), TOP_K)
    row_tok = jnp.zeros((n_blocks * MOE_BLOCK,), jnp.int32).at[dest].set(flat_tok)
    block_e = jnp.minimum(jnp.searchsorted(pend, jnp.arange(n_blocks, dtype=jnp.int32) * MOE_BLOCK, side="right"),
                          N_EXPERTS - 1).astype(jnp.int32)
    n_used = (pend[-1:] // MOE_BLOCK).astype(jnp.int32)
    return row_tok, dest.reshape(n_tokens, TOP_K), block_e, n_used


def _final_norm_kernel(x_ref, gain_ref, o_ref):
    o_ref[...] = _rms(x_ref[...], gain_ref[...])


def final_rms_norm(x, gain, *, tm=512):
    t, d = x.shape
    return pl.pallas_call(
        _final_norm_kernel,
        grid=(t // tm,),
        in_specs=[pl.BlockSpec((tm, d), lambda i: (i, 0)), pl.BlockSpec((1, d), lambda i: (0, 0))],
        out_specs=pl.BlockSpec((tm, d), lambda i: (i, 0)),
        out_shape=jax.ShapeDtypeStruct((t, d), F32),
        compiler_params=_cparams("arbitrary"),
        name="final_norm",
    )(x, gain.reshape(1, d))


def _pad_cols(w, width):
    return jnp.pad(w, ((0, 0), (0, width - w.shape[1])))


def projection_weight(w_in, w_gate):
    lru_width = (w_in.shape[1] - IN_FIXED_WIDTH) // 2
    o = np.cumsum((0,) + IN_FIXED_WIDTHS + (lru_width, lru_width))
    c_q, c_kv, k_rope, qkv, z, ab, lru_x, lru_y = (w_in[:, int(o[k]):int(o[k + 1])] for k in range(8))
    parts = [w_gate, qkv, z, lru_x, lru_y, c_q, c_kv, _pad_cols(k_rope, LANES), _pad_cols(ab, LANES)]
    return jnp.concatenate(parts, axis=1).astype(BF16)


def mla_weights(w_uq, w_ukv):
    d_q, d_kv = w_uq.shape[0], w_ukv.shape[0]
    wq = w_uq.reshape(d_q, MLA_HEADS, MLA_NOPE + MLA_ROPE)
    wq = jnp.pad(wq, ((0, 0), (0, 0), (0, MLA_QK_PAD - MLA_NOPE - MLA_ROPE))).reshape(d_q, MLA_HEADS * MLA_QK_PAD)
    wkv = w_ukv.reshape(d_kv, MLA_HEADS, MLA_NOPE + MLA_V)
    wk = wkv[:, :, :MLA_NOPE].reshape(d_kv, MLA_HEADS * MLA_NOPE)
    wv = wkv[:, :, MLA_NOPE:].reshape(d_kv, MLA_HEADS * MLA_V)
    return wq.astype(BF16), wk.astype(BF16), wv.astype(BF16)


def _gdn_gate_rows(a_log, dt_bias):
    zeros = jnp.zeros((GDN_HEADS,), F32)
    row = lambda v: _pad_cols(jnp.concatenate([v[0], zeros, v[1], zeros])[None, :], LANES)
    return row(a_log.astype(F32)), row(dt_bias.astype(F32))


def kernel(x, c, positions, ada_w, ada_b, norm_mix, norm_ffn, w_in, mla_q_norm, mla_w_uq, mla_kv_norm,
           mla_w_ukv, gdn_conv, gdn_a_log, gdn_dt_bias, gdn_out_norm, lru_conv_w, lru_conv_b, lru_w_a,
           lru_b_a, lru_w_i, lru_b_i, lru_lambda, w_branch_a, w_branch_b, w_branch_c, w_gate, w_out,
           ffn_w1, ffn_w3, ffn_w2, moe_router, moe_router_b, moe_w1, moe_w3, moe_w2, final_norm):
    bsz, seq, d = x.shape
    t = bsz * seq
    depth = ada_w.shape[0]
    xf = x.reshape(t, d)
    mod = ada_modulation(c, ada_w, ada_b).reshape(depth, bsz, 6, 1, d)
    cos, sin = rope_tables(positions)
    for layer in range(depth):
        shift_m, scale_m, gate_m, shift_f, scale_f, gate_f = (mod[layer, :, k] for k in range(6))
        p = norm_projection(xf, norm_mix[layer], scale_m, shift_m, projection_weight(w_in[layer], w_gate[layer]), seq)
        wq, wk, wv = mla_weights(mla_w_uq[layer], mla_w_ukv[layer])
        q = mla_q(p, mla_q_norm[layer], wq, cos, sin)
        k, v = mla_kv(p, mla_kv_norm[layer], wk, wv, cos, sin)
        o_a = flash_attention(q, k, v, bsz, seq)
        qkv = gdn_prep(p, gdn_conv[layer], seq)
        a_log_row, dt_row = _gdn_gate_rows(gdn_a_log[layer], gdn_dt_bias[layer])
        o_f, o_r = gdn_chunk_scan(qkv, p, a_log_row, dt_row, bsz, seq)
        o_b = gdn_output(o_f, o_r, p, gdn_out_norm[layer])
        o_c = rg_lru(p, lru_conv_w[layer], lru_conv_b[layer], lru_w_a[layer], lru_b_a[layer], lru_w_i[layer],
                     lru_b_i[layer], lru_lambda[layer], seq)
        merged = merge_branches(o_a, o_b, o_c, w_branch_a[layer].astype(BF16), w_branch_b[layer].astype(BF16),
                                w_branch_c[layer].astype(BF16), p)
        xf = matmul_gated_residual(merged, w_out[layer].astype(BF16), xf, gate_m, seq, tm=512, tn=d)
        j = layer // 2
        if layer % 2 == 0:
            act = norm_swiglu_up(xf, norm_ffn[layer], scale_f, shift_f, ffn_w1[j].astype(BF16),
                                 ffn_w3[j].astype(BF16), seq)
            xf = matmul_gated_residual(act, ffn_w2[j].astype(BF16), xf, gate_f, seq, tm=1024, tn=256)
        else:
            h, route = norm_router(xf, norm_ffn[layer], scale_f, shift_f, moe_router[j], moe_router_b[j], seq)
            row_tok, dest, block_e, n_used = routing_plan(route, t)
            act = expert_swiglu_up(block_e, n_used, expert_dispatch(n_used, row_tok, h), moe_w1[j], moe_w3[j])
            y = expert_down(block_e, n_used, act, moe_w2[j].astype(BF16))
            xf = combine_residual(xf, y, dest, route, gate_f, seq)
    return final_rms_norm(xf, final_norm).reshape(bsz, seq, d)
```

```python
import functools
import math

import jax
import jax.numpy as jnp
import numpy as np
from jax import lax
from jax.experimental import pallas as pl
from jax.experimental.pallas import tpu as pltpu

F32 = jnp.float32
BF16 = jnp.bfloat16
HIGHEST = lax.Precision.HIGHEST

LANES = 128
SUBLANES = 8
VMEM_LIMIT_BYTES = 56 * 1024 * 1024

NORM_EPS = 1e-6
ROPE_BASE = 10000.0
MLA_HEADS = 8
MLA_NOPE = 128
MLA_ROPE = 64
MLA_V = 128
MLA_QK_PAD = 256
GDN_HEADS = 8
GDN_DIM = 128
GDN_CHUNK = 64
LRU_BLOCKS = 8
LRU_C = 8.0
N_EXPERTS = 8
TOP_K = 2
MOE_BLOCK = 512
IN_FIXED_WIDTHS = (512, 256, MLA_ROPE, 3 * GDN_HEADS * GDN_DIM, GDN_HEADS * GDN_DIM, 4 * GDN_HEADS)
IN_FIXED_WIDTH = sum(IN_FIXED_WIDTHS)

P_GATE, P_QKV, P_Z, P_LRUX, P_LRUY, P_CQ, P_CKV, P_KROPE, P_AB, P_TOTAL = 0, 48, 72, 80, 88, 96, 100, 102, 103, 104


def _cparams(*semantics):
    return pltpu.CompilerParams(dimension_semantics=semantics, vmem_limit_bytes=VMEM_LIMIT_BYTES)


def _silu(x):
    return x * jax.nn.sigmoid(x)


def _softplus(x):
    return jnp.maximum(x, 0.0) + jnp.log1p(jnp.exp(-jnp.abs(x)))


def _bdot(a, b):
    return jnp.dot(a.astype(BF16), b.astype(BF16), preferred_element_type=F32)


def _bdot_nt(a, b):
    return lax.dot_general(a.astype(BF16), b.astype(BF16), (((1,), (1,)), ((), ())),
                           preferred_element_type=F32)


def _bdot_tn(a, b):
    return lax.dot_general(a.astype(BF16), b.astype(BF16), (((0,), (0,)), ((), ())),
                           preferred_element_type=F32)


def _fdot(a, b):
    return jnp.dot(a, b, preferred_element_type=F32, precision=HIGHEST)


def _ada_kernel(c_ref, w_ref, b_ref, o_ref):
    cond = _silu(c_ref[...])
    o_ref[...] = _bdot(cond, w_ref[...]) + b_ref[...]


def ada_modulation(c, ada_w, ada_b, *, tn=1024):
    n_layers, d, n = ada_w.shape
    bsz = c.shape[0]
    return pl.pallas_call(
        _ada_kernel,
        grid=(n_layers, n // tn),
        in_specs=[
            pl.BlockSpec((bsz, d), lambda l, j: (0, 0)),
            pl.BlockSpec((None, d, tn), lambda l, j: (l, 0, j)),
            pl.BlockSpec((None, 1, tn), lambda l, j: (l, 0, j)),
        ],
        out_specs=pl.BlockSpec((None, bsz, tn), lambda l, j: (l, 0, j)),
        out_shape=jax.ShapeDtypeStruct((n_layers, bsz, n), F32),
        compiler_params=_cparams("arbitrary", "arbitrary"),
        name="ada_modulation",
    )(c, ada_w, ada_b.reshape(n_layers, 1, n))


def _norm_mod(x, gain, scale, shift):
    y = x * lax.rsqrt(jnp.mean(x * x, axis=-1, keepdims=True) + NORM_EPS)
    return (y * gain) * (1.0 + scale) + shift


NORM_SLAB_ROWS = 256


def _store_norm_mod(x_ref, gain_ref, scale_ref, shift_ref, h_ref):
    gain, scale, shift = gain_ref[...], scale_ref[...], shift_ref[...]
    for r in range(0, x_ref.shape[0], NORM_SLAB_ROWS):
        rows = slice(r, r + NORM_SLAB_ROWS)
        h_ref[rows, :] = _norm_mod(x_ref[rows, :], gain, scale, shift).astype(BF16)


def _proj_kernel(x_ref, gain_ref, scale_ref, shift_ref, w_ref, o_ref, h_ref):
    @pl.when(pl.program_id(1) == 0)
    def _():
        _store_norm_mod(x_ref, gain_ref, scale_ref, shift_ref, h_ref)

    o_ref[...] = jnp.dot(h_ref[...], w_ref[...], preferred_element_type=F32)


def _row_specs(d, tm, tiles_per_batch):
    return [
        pl.BlockSpec((tm, d), lambda i, j: (i, 0)),
        pl.BlockSpec((1, d), lambda i, j: (0, 0)),
        pl.BlockSpec((None, 1, d), lambda i, j: (i // tiles_per_batch, 0, 0)),
        pl.BlockSpec((None, 1, d), lambda i, j: (i // tiles_per_batch, 0, 0)),
    ]


def norm_projection(x, gain, scale, shift, w, seq, *, tm=1024, tn=1024):
    t, d = x.shape
    n = w.shape[1]
    tm = min(tm, seq)
    return pl.pallas_call(
        _proj_kernel,
        grid=(t // tm, n // tn),
        in_specs=_row_specs(d, tm, seq // tm) + [pl.BlockSpec((d, tn), lambda i, j: (0, j))],
        out_specs=pl.BlockSpec((tm, tn), lambda i, j: (i, j)),
        out_shape=jax.ShapeDtypeStruct((t, n), F32),
        scratch_shapes=[pltpu.VMEM((tm, d), BF16)],
        compiler_params=_cparams("arbitrary", "arbitrary"),
        name="norm_projection",
    )(x, gain.reshape(1, d), scale, shift, w)


def _rope_kernel(pos_ref, freq_ref, cos_ref, sin_ref):
    ang = pos_ref[...] * freq_ref[...]
    live = lax.broadcasted_iota(jnp.int32, ang.shape, 1) < MLA_ROPE
    cos_ref[...] = jnp.where(live, jnp.cos(ang), 0.0)
    sin_ref[...] = jnp.where(live, jnp.sin(ang), 0.0)


def rope_tables(positions, *, tm=1024):
    t = positions.size
    inv_freq = ROPE_BASE ** (-np.arange(0, MLA_ROPE, 2, dtype=np.float32) / MLA_ROPE)
    freq = np.zeros((1, LANES), np.float32)
    freq[0, :MLA_ROPE] = np.tile(inv_freq, 2)
    pos = positions.astype(F32).reshape(t, 1)
    return pl.pallas_call(
        _rope_kernel,
        grid=(t // tm,),
        in_specs=[pl.BlockSpec((tm, 1), lambda i: (i, 0)), pl.BlockSpec((1, LANES), lambda i: (0, 0))],
        out_specs=[pl.BlockSpec((tm, LANES), lambda i: (i, 0))] * 2,
        out_shape=[jax.ShapeDtypeStruct((t, LANES), F32)] * 2,
        compiler_params=_cparams("arbitrary"),
        name="rope_tables",
    )(pos, jnp.asarray(freq))


def _rope_half(t, cos, sin):
    lane = lax.broadcasted_iota(jnp.int32, t.shape, 1)
    rot = jnp.where(lane < MLA_ROPE // 2, -pltpu.roll(t, LANES - MLA_ROPE // 2, 1), pltpu.roll(t, MLA_ROPE // 2, 1))
    return t * cos + rot * sin


def _rms(x, gain):
    return x * lax.rsqrt(jnp.mean(x * x, axis=-1, keepdims=True) + NORM_EPS) * gain


def _mla_q_kernel(cq_ref, gain_ref, w_ref, cos_ref, sin_ref, o_ref):
    scale = (MLA_NOPE + MLA_ROPE) ** -0.5 * math.log2(math.e)
    q = _bdot(_rms(cq_ref[...], gain_ref[...]), w_ref[...])
    cos, sin = cos_ref[...], sin_ref[...]
    for h in range(MLA_HEADS):
        lo = h * MLA_QK_PAD
        o_ref[:, lo:lo + LANES] = (q[:, lo:lo + LANES] * scale).astype(BF16)
        o_ref[:, lo + LANES:lo + 2 * LANES] = (_rope_half(q[:, lo + LANES:lo + 2 * LANES], cos, sin) * scale).astype(BF16)


def mla_q(p, gain, w_q, cos, sin, *, tm=512):
    t = p.shape[0]
    lora = gain.shape[0]
    n = w_q.shape[1]
    return pl.pallas_call(
        _mla_q_kernel,
        grid=(t // tm,),
        in_specs=[
            pl.BlockSpec((tm, lora), lambda i: (i, P_CQ * LANES // lora)),
            pl.BlockSpec((1, lora), lambda i: (0, 0)),
            pl.BlockSpec((lora, n), lambda i: (0, 0)),
            pl.BlockSpec((tm, LANES), lambda i: (i, 0)),
            pl.BlockSpec((tm, LANES), lambda i: (i, 0)),
        ],
        out_specs=pl.BlockSpec((tm, n), lambda i: (i, 0)),
        out_shape=jax.ShapeDtypeStruct((t, n), BF16),
        compiler_params=_cparams("arbitrary"),
        name="mla_q",
    )(p, gain.reshape(1, lora), w_q, cos, sin)


def _mla_kv_kernel(ckv_ref, kr_ref, gain_ref, wk_ref, wv_ref, cos_ref, sin_ref, k_ref, v_ref):
    n = _rms(ckv_ref[...], gain_ref[...]).astype(BF16)
    k_nope = jnp.dot(n, wk_ref[...], preferred_element_type=F32)
    v = jnp.dot(n, wv_ref[...], preferred_element_type=F32)
    k_rope = _rope_half(kr_ref[...], cos_ref[...], sin_ref[...]).astype(BF16)
    ones_lane = (lax.broadcasted_iota(jnp.int32, k_rope.shape, 1) == 0).astype(BF16)
    for h in range(MLA_HEADS):
        lo = h * MLA_QK_PAD
        k_ref[:, lo:lo + LANES] = k_nope[:, h * MLA_NOPE:(h + 1) * MLA_NOPE].astype(BF16)
        k_ref[:, lo + LANES:lo + 2 * LANES] = k_rope
        v_ref[:, lo:lo + LANES] = v[:, h * MLA_V:(h + 1) * MLA_V].astype(BF16)
        v_ref[:, lo + LANES:lo + 2 * LANES] = ones_lane


def mla_kv(p, gain, w_k, w_v, cos, sin, *, tm=512):
    t = p.shape[0]
    lora = gain.shape[0]
    return pl.pallas_call(
        _mla_kv_kernel,
        grid=(t // tm,),
        in_specs=[
            pl.BlockSpec((tm, lora), lambda i: (i, P_CKV * LANES // lora)),
            pl.BlockSpec((tm, LANES), lambda i: (i, P_KROPE)),
            pl.BlockSpec((1, lora), lambda i: (0, 0)),
            pl.BlockSpec(w_k.shape, lambda i: (0, 0)),
            pl.BlockSpec(w_v.shape, lambda i: (0, 0)),
            pl.BlockSpec((tm, LANES), lambda i: (i, 0)),
            pl.BlockSpec((tm, LANES), lambda i: (i, 0)),
        ],
        out_specs=[
            pl.BlockSpec((tm, MLA_HEADS * MLA_QK_PAD), lambda i: (i, 0)),
            pl.BlockSpec((tm, MLA_HEADS * MLA_QK_PAD), lambda i: (i, 0)),
        ],
        out_shape=[
            jax.ShapeDtypeStruct((t, MLA_HEADS * MLA_QK_PAD), BF16),
            jax.ShapeDtypeStruct((t, MLA_HEADS * MLA_QK_PAD), BF16),
        ],
        compiler_params=_cparams("arbitrary"),
        name="mla_kv",
    )(p, p, gain.reshape(1, lora), w_k, w_v, cos, sin)


def _flash_kernel(q_ref, k_ref, v_ref, o_ref, m_ref, acc_ref, *, tk):
    m_ref[...] = jnp.full(m_ref.shape, -jnp.inf, F32)
    acc_ref[...] = jnp.zeros(acc_ref.shape, F32)
    q = q_ref[...]

    def body(i, _):
        rows = pl.ds(pl.multiple_of(i * tk, tk), tk)
        s = lax.dot_general(q, k_ref[rows, :], (((1,), (1,)), ((), ())), preferred_element_type=F32)
        m_prev = m_ref[...]
        m_new = jnp.maximum(m_prev, jnp.max(s, axis=-1, keepdims=True))
        p = jnp.exp2(s - m_new).astype(BF16)
        acc_ref[...] = jnp.exp2(m_prev - m_new) * acc_ref[...] + jnp.dot(p, v_ref[rows, :],
                                                                         preferred_element_type=F32)
        m_ref[...] = m_new
        return 0

    lax.fori_loop(0, k_ref.shape[0] // tk, body, 0, unroll=True)
    acc = acc_ref[...]
    o_ref[...] = (acc[:, :MLA_V] / acc[:, MLA_V:MLA_V + 1]).astype(o_ref.dtype)


def flash_attention(q, k, v, bsz, seq, *, tq=512, tk=1024):
    t = q.shape[0]
    tq, tk = min(tq, seq), min(tk, seq)
    nq = seq // tq
    return pl.pallas_call(
        functools.partial(_flash_kernel, tk=tk),
        grid=(bsz, MLA_HEADS, nq),
        in_specs=[
            pl.BlockSpec((tq, MLA_QK_PAD), lambda b, h, qi: (b * nq + qi, h)),
            pl.BlockSpec((seq, MLA_QK_PAD), lambda b, h, qi: (b, h)),
            pl.BlockSpec((seq, MLA_QK_PAD), lambda b, h, qi: (b, h)),
        ],
        out_specs=pl.BlockSpec((tq, MLA_V), lambda b, h, qi: (b * nq + qi, h)),
        out_shape=jax.ShapeDtypeStruct((t, MLA_HEADS * MLA_V), BF16),
        scratch_shapes=[pltpu.VMEM((tq, 1), F32), pltpu.VMEM((tq, MLA_QK_PAD), F32)],
        compiler_params=_cparams("arbitrary", "arbitrary", "arbitrary"),
        name="flash_attention",
    )(q, k, v)


def _conv4(main_ref, prev_ref, next_ref, w_ref, pad_ref, first, last):
    tm = main_ref.shape[0]
    x = main_ref[...]
    pad_ref[0:SUBLANES, :] = jnp.where(first, 0.0, prev_ref[...])
    pad_ref[SUBLANES:SUBLANES + tm, :] = x
    pad_ref[SUBLANES + tm:2 * SUBLANES + tm, :] = jnp.where(last, 0.0, next_ref[...])
    w = w_ref[...]
    return (w[0:1] * pad_ref[SUBLANES - 1:SUBLANES - 1 + tm, :] + w[1:2] * x
            + w[2:3] * pad_ref[SUBLANES + 1:SUBLANES + 1 + tm, :]
            + w[3:4] * pad_ref[SUBLANES + 2:SUBLANES + 2 + tm, :])


def _halo_specs(tm, width, col, n_row8, grid_rank):
    r8 = tm // SUBLANES
    if grid_rank == 1:
        return [
            pl.BlockSpec((tm, width), lambda i: (i, col)),
            pl.BlockSpec((SUBLANES, width), lambda i: (jnp.maximum(i * r8 - 1, 0), col)),
            pl.BlockSpec((SUBLANES, width), lambda i: (jnp.minimum((i + 1) * r8, n_row8 - 1), col)),
        ]
    return [
        pl.BlockSpec((tm, width), lambda i, j: (i, col + j)),
        pl.BlockSpec((SUBLANES, width), lambda i, j: (jnp.maximum(i * r8 - 1, 0), col + j)),
        pl.BlockSpec((SUBLANES, width), lambda i, j: (jnp.minimum((i + 1) * r8, n_row8 - 1), col + j)),
    ]


def _gdn_prep_kernel(main_ref, prev_ref, next_ref, w_ref, o_ref, pad_ref, *, tiles_per_batch):
    i, j = pl.program_id(0), pl.program_id(1)
    first = (i % tiles_per_batch) == 0
    last = (i % tiles_per_batch) == tiles_per_batch - 1
    y = _silu(_conv4(main_ref, prev_ref, next_ref, w_ref, pad_ref, first, last))

    @pl.when(j < 2)
    def _():
        post = jnp.where(j == 0, GDN_DIM ** -0.5, 1.0)
        for h in range(GDN_HEADS):
            seg = y[:, h * GDN_DIM:(h + 1) * GDN_DIM]
            inv = lax.rsqrt(jnp.sum(seg * seg, axis=-1, keepdims=True) + NORM_EPS)
            o_ref[:, h * GDN_DIM:(h + 1) * GDN_DIM] = seg * (inv * post)

    @pl.when(j == 2)
    def _():
        o_ref[...] = y


def gdn_prep(p, conv_w, seq, *, tm=512):
    t = p.shape[0]
    width = GDN_HEADS * GDN_DIM
    col = P_QKV * LANES // width
    return pl.pallas_call(
        functools.partial(_gdn_prep_kernel, tiles_per_batch=seq // tm),
        grid=(t // tm, 3),
        in_specs=_halo_specs(tm, width, col, t // SUBLANES, 2) + [pl.BlockSpec((4, width), lambda i, j: (0, j))],
        out_specs=pl.BlockSpec((tm, width), lambda i, j: (i, j)),
        out_shape=jax.ShapeDtypeStruct((t, 3 * width), F32),
        scratch_shapes=[pltpu.VMEM((tm + 2 * SUBLANES, width), F32)],
        compiler_params=_cparams("arbitrary", "arbitrary"),
        name="gdn_prep",
    )(p, p, p, conv_w)


def _unit_triangular_inverse(low, dot):
    c = low[0].shape[0]
    heads = range(len(low))
    row = lax.broadcasted_iota(jnp.int32, (c, c), 0)
    col = lax.broadcasted_iota(jnp.int32, (c, c), 1)
    eye = (row == col).astype(F32)
    inv = [eye - jnp.where((row ^ 1) == col, low[h], 0.0) for h in heads]
    shift = 1
    while (2 << shift) <= c:
        off_block = ((row >> shift) ^ 1) == (col >> shift)
        blk = [jnp.where(off_block, low[h], 0.0) for h in heads]
        half = [dot(inv[h], blk[h]) for h in heads]
        inv = [inv[h] - dot(half[h], inv[h]) for h in heads]
        shift += 1
    return inv


def _gdn_chunk_kernel(qf_ref, kf_ref, vf_ref, abf_ref, qr_ref, kr_ref, vr_ref, abr_ref, alog_ref, dt_ref,
                      of_ref, or_ref, state_ref):
    c = qf_ref.shape[0]

    @pl.when(pl.program_id(1) == 0)
    def _():
        state_ref[...] = jnp.zeros(state_ref.shape, F32)

    row = lax.broadcasted_iota(jnp.int32, (c, c), 0)
    col = lax.broadcasted_iota(jnp.int32, (c, c), 1)
    eye_l = (lax.broadcasted_iota(jnp.int32, (LANES, LANES), 0)
             == lax.broadcasted_iota(jnp.int32, (LANES, LANES), 1)).astype(F32)
    sl = [slice(h * GDN_DIM, (h + 1) * GDN_DIM) for h in range(GDN_HEADS)]

    q, k, v, g_col, g_row, beta, incl, strict, g_last = [], [], [], [], [], [], [], [], []
    for reverse, (q_ref, k_ref, v_ref, ab_ref) in enumerate(((qf_ref, kf_ref, vf_ref, abf_ref),
                                                              (qr_ref, kr_ref, vr_ref, abr_ref))):
        ab = ab_ref[...]
        log_alpha = -jnp.exp(alog_ref[...]) * _softplus(ab + dt_ref[...])
        beta_all = jax.nn.sigmoid(ab)
        cum = ((col >= row) if reverse else (col <= row)).astype(F32)
        g_all = _fdot(cum, log_alpha)
        g_rows = lax.dot_general(eye_l, g_all, (((1,), (1,)), ((), ())), preferred_element_type=F32,
                                 precision=HIGHEST)
        a_lane = 2 * GDN_HEADS * reverse
        b_lane = a_lane + GDN_HEADS
        last = 0 if reverse else c - 1
        for h in range(GDN_HEADS):
            q.append(q_ref[:, sl[h]])
            k.append(k_ref[:, sl[h]])
            v.append(v_ref[:, sl[h]])
            g_col.append(g_all[:, a_lane + h:a_lane + h + 1])
            g_row.append(g_rows[a_lane + h:a_lane + h + 1, :])
            beta.append(beta_all[:, b_lane + h:b_lane + h + 1])
            incl.append((row <= col) if reverse else (row >= col))
            strict.append((row < col) if reverse else (row > col))
            g_last.append(g_all[last:last + 1, a_lane + h:a_lane + h + 1])

    chains = range(2 * GDN_HEADS)
    state = [state_ref[n] for n in chains]
    decay = [jnp.exp(jnp.where(incl[n], g_col[n] - g_row[n], -jnp.inf)) for n in chains]
    eg = [jnp.exp(g_col[n]) for n in chains]
    kb = [k[n] * beta[n] for n in chains]
    on_k = [_bdot_nt(jnp.concatenate([kb[n], q[n]], axis=0), k[n]) for n in chains]
    low = [jnp.where(strict[n], on_k[n][:c] * decay[n], 0.0) for n in chains]
    qk = [on_k[n][c:] * decay[n] for n in chains]
    inv = _unit_triangular_inverse(low, _bdot)
    sol = [_bdot(inv[n], jnp.concatenate([v[n] * beta[n], kb[n] * eg[n]], axis=1)) for n in chains]
    on_s = [_bdot(jnp.concatenate([sol[n][:, GDN_DIM:], q[n] * eg[n]], axis=0), state[n]) for n in chains]
    v_new = [sol[n][:, :GDN_DIM] - on_s[n][:c] for n in chains]
    out = [on_s[n][c:] + _bdot(qk[n], v_new[n]) for n in chains]
    new_state = [state[n] * jnp.exp(g_last[n]) + _bdot_tn(k[n] * jnp.exp(g_last[n] - g_col[n]), v_new[n])
                 for n in chains]
    for h in range(GDN_HEADS):
        of_ref[:, sl[h]] = out[h]
        or_ref[:, sl[h]] = out[GDN_HEADS + h]
    for n in chains:
        state_ref[n] = new_state[n]


def gdn_chunk_scan(qkv, p, a_log_row, dt_row, bsz, seq):
    t = qkv.shape[0]
    c = GDN_CHUNK
    n = seq // c
    width = GDN_HEADS * GDN_DIM
    fwd = lambda b, i: b * n + i
    rev = lambda b, i: b * n + n - 1 - i
    chunk = lambda rows, col: pl.BlockSpec((c, width), lambda b, i: (rows(b, i), col))
    gates = lambda rows: pl.BlockSpec((c, LANES), lambda b, i: (rows(b, i), P_AB))
    param = pl.BlockSpec((1, LANES), lambda b, i: (0, 0))
    return pl.pallas_call(
        _gdn_chunk_kernel,
        grid=(bsz, n),
        in_specs=[chunk(fwd, 0), chunk(fwd, 1), chunk(fwd, 2), gates(fwd),
                  chunk(rev, 0), chunk(rev, 1), chunk(rev, 2), gates(rev), param, param],
        out_specs=[chunk(fwd, 0), chunk(rev, 0)],
        out_shape=[jax.ShapeDtypeStruct((t, width), F32)] * 2,
        scratch_shapes=[pltpu.VMEM((2 * GDN_HEADS, GDN_DIM, GDN_DIM), F32)],
        compiler_params=_cparams("arbitrary", "arbitrary"),
        name="gdn_chunk_scan",
    )(qkv, qkv, qkv, p, qkv, qkv, qkv, p, a_log_row, dt_row)


def _gdn_out_kernel(of_ref, ob_ref, z_ref, gain_ref, o_ref):
    o = of_ref[...] + ob_ref[...]
    z = z_ref[...]
    gain = gain_ref[...]
    for h in range(GDN_HEADS):
        sl = slice(h * GDN_DIM, (h + 1) * GDN_DIM)
        o_ref[:, sl] = (_rms(o[:, sl], gain) * _silu(z[:, sl])).astype(BF16)


def gdn_output(o_f, o_b, p, out_norm, *, tm=512):
    t, width = o_f.shape
    return pl.pallas_call(
        _gdn_out_kernel,
        grid=(t // tm,),
        in_specs=[
            pl.BlockSpec((tm, width), lambda i: (i, 0)),
            pl.BlockSpec((tm, width), lambda i: (i, 0)),
            pl.BlockSpec((tm, width), lambda i: (i, P_Z * LANES // width)),
            pl.BlockSpec((1, GDN_DIM), lambda i: (0, 0)),
        ],
        out_specs=pl.BlockSpec((tm, width), lambda i: (i, 0)),
        out_shape=jax.ShapeDtypeStruct((t, width), BF16),
        compiler_params=_cparams("arbitrary"),
        name="gdn_output",
    )(o_f, o_b, p, out_norm.reshape(1, GDN_DIM))


def _lru_scan_tile(a, u, carry_ref, a_ref, b_ref, h_ref, reverse):
    tm, width = a.shape
    groups = tm // SUBLANES
    a = a.reshape(groups, SUBLANES, width)
    u = u.reshape(groups, SUBLANES, width)
    sub = lax.broadcasted_iota(jnp.int32, a.shape, 1)
    shift = 1
    while shift < SUBLANES:
        if reverse:
            ok = sub < SUBLANES - shift
            a_sh, u_sh = pltpu.roll(a, SUBLANES - shift, 1), pltpu.roll(u, SUBLANES - shift, 1)
        else:
            ok = sub >= shift
            a_sh, u_sh = pltpu.roll(a, shift, 1), pltpu.roll(u, shift, 1)
        u = a * jnp.where(ok, u_sh, 0.0) + u
        a = a * jnp.where(ok, a_sh, 1.0)
        shift *= 2
    a_ref[...] = a.reshape(tm, width)
    b_ref[...] = u.reshape(tm, width)
    edge = 0 if reverse else SUBLANES - 1

    def body(g, h_in):
        r0 = pl.multiple_of((groups - 1 - g if reverse else g) * SUBLANES, SUBLANES)
        h = a_ref[pl.ds(r0, SUBLANES), :] * h_in + b_ref[pl.ds(r0, SUBLANES), :]
        h_ref[pl.ds(r0, SUBLANES), :] = h
        return jnp.broadcast_to(h[edge:edge + 1, :], h.shape)

    carry_ref[...] = lax.fori_loop(0, groups, body, carry_ref[...], unroll=4)


def _lru_gates(xc, w_ref, ba_ref, bi_ref, lam_ref):
    blk = xc.shape[1] // LRU_BLOCKS
    parts = [_bdot(xc[:, n * blk:(n + 1) * blk], w_ref[n]) for n in range(LRU_BLOCKS)]
    r = jax.nn.sigmoid(jnp.concatenate([p[:, :blk] for p in parts], axis=1) + ba_ref[...])
    gate_i = jax.nn.sigmoid(jnp.concatenate([p[:, blk:] for p in parts], axis=1) + bi_ref[...])
    log_a = -LRU_C * r * _softplus(-lam_ref[...])
    a = jnp.exp(log_a)
    return a, xc * gate_i * jnp.sqrt(1.0 - a * a)


def _lru_fwd_kernel(main_ref, prev_ref, next_ref, cw_ref, cb_ref, w_ref, ba_ref, bi_ref, lam_ref,
                    h_ref, pad_ref, carry_ref, a_ref, b_ref, *, tiles_per_batch):
    i = pl.program_id(0) % tiles_per_batch

    @pl.when(i == 0)
    def _():
        carry_ref[...] = jnp.zeros(carry_ref.shape, F32)

    xc = _conv4(main_ref, prev_ref, next_ref, cw_ref, pad_ref, i == 0, i == tiles_per_batch - 1) + cb_ref[...]
    a, u = _lru_gates(xc, w_ref, ba_ref, bi_ref, lam_ref)
    _lru_scan_tile(a, u, carry_ref, a_ref, b_ref, h_ref, False)


def _lru_bwd_kernel(main_ref, prev_ref, next_ref, cw_ref, cb_ref, w_ref, ba_ref, bi_ref, lam_ref, hf_ref, y_ref,
                    o_ref, pad_ref, carry_ref, a_ref, b_ref, h_ref, *, tiles_per_batch):
    i = pl.program_id(0) % tiles_per_batch

    @pl.when(i == 0)
    def _():
        carry_ref[...] = jnp.zeros(carry_ref.shape, F32)

    xc = _conv4(main_ref, prev_ref, next_ref, cw_ref, pad_ref, i == tiles_per_batch - 1, i == 0) + cb_ref[...]
    a, u = _lru_gates(xc, w_ref, ba_ref, bi_ref, lam_ref)
    _lru_scan_tile(a, u, carry_ref, a_ref, b_ref, h_ref, True)
    o_ref[...] = ((hf_ref[...] + h_ref[...]) * jax.nn.gelu(y_ref[...], approximate=True)).astype(BF16)


def _lru_param_specs(width, blk):
    return [
        pl.BlockSpec((4, width), lambda i: (0, 0)),
        pl.BlockSpec((1, width), lambda i: (0, 0)),
        pl.BlockSpec((LRU_BLOCKS, blk, 2 * blk), lambda i: (0, 0, 0)),
        pl.BlockSpec((1, width), lambda i: (0, 0)),
        pl.BlockSpec((1, width), lambda i: (0, 0)),
        pl.BlockSpec((1, width), lambda i: (0, 0)),
    ]


def _lru_scratch(tm, width):
    return [pltpu.VMEM((tm + 2 * SUBLANES, width), F32), pltpu.VMEM((SUBLANES, width), F32),
            pltpu.VMEM((tm, width), F32), pltpu.VMEM((tm, width), F32)]


def rg_lru(p, conv_w, conv_b, w_a, b_a, w_i, b_i, lam, seq, *, tm=256):
    t = p.shape[0]
    width = conv_w.shape[1]
    blk = width // LRU_BLOCKS
    n_tiles = t // tm
    col_x, col_y = P_LRUX * LANES // width, P_LRUY * LANES // width
    w_dir = [jnp.concatenate([w_a[d], w_i[d]], axis=-1).astype(BF16) for d in range(2)]
    params = lambda d: (conv_w, conv_b.reshape(1, width), w_dir[d], b_a[d].reshape(1, width),
                        b_i[d].reshape(1, width), lam[d].reshape(1, width))
    h_f = pl.pallas_call(
        functools.partial(_lru_fwd_kernel, tiles_per_batch=seq // tm),
        grid=(n_tiles,),
        in_specs=_halo_specs(tm, width, col_x, t // SUBLANES, 1) + _lru_param_specs(width, blk),
        out_specs=pl.BlockSpec((tm, width), lambda i: (i, 0)),
        out_shape=jax.ShapeDtypeStruct((t, width), F32),
        scratch_shapes=_lru_scratch(tm, width),
        compiler_params=_cparams("arbitrary"),
        name="rg_lru_fwd",
    )(p, p, p, *params(0))
    r8 = tm // SUBLANES
    n_row8 = t // SUBLANES
    rev = lambda i: n_tiles - 1 - i
    halo_rev = [
        pl.BlockSpec((tm, width), lambda i: (rev(i), col_x)),
        pl.BlockSpec((SUBLANES, width), lambda i: (jnp.maximum(rev(i) * r8 - 1, 0), col_x)),
        pl.BlockSpec((SUBLANES, width), lambda i: (jnp.minimum((rev(i) + 1) * r8, n_row8 - 1), col_x)),
    ]
    return pl.pallas_call(
        functools.partial(_lru_bwd_kernel, tiles_per_batch=seq // tm),
        grid=(n_tiles,),
        in_specs=halo_rev + _lru_param_specs(width, blk) + [
            pl.BlockSpec((tm, width), lambda i: (rev(i), 0)),
            pl.BlockSpec((tm, width), lambda i: (rev(i), col_y)),
        ],
        out_specs=pl.BlockSpec((tm, width), lambda i: (rev(i), 0)),
        out_shape=jax.ShapeDtypeStruct((t, width), BF16),
        scratch_shapes=_lru_scratch(tm, width) + [pltpu.VMEM((tm, width), F32)],
        compiler_params=_cparams("arbitrary"),
        name="rg_lru_bwd",
    )(p, p, p, *params(1), h_f, p)


def _merge_kernel(oa_ref, ob_ref, oc_ref, wa_ref, wb_ref, wc_ref, ga_ref, gb_ref, gc_ref, o_ref):
    acc = jax.nn.sigmoid(ga_ref[...]) * jnp.dot(oa_ref[...], wa_ref[...], preferred_element_type=F32)
    acc += jax.nn.sigmoid(gb_ref[...]) * jnp.dot(ob_ref[...], wb_ref[...], preferred_element_type=F32)
    acc += jax.nn.sigmoid(gc_ref[...]) * jnp.dot(oc_ref[...], wc_ref[...], preferred_element_type=F32)
    o_ref[...] = acc.astype(BF16)


def merge_branches(o_a, o_b, o_c, w_a, w_b, w_c, p, *, tm=256, tn=2048):
    t = o_a.shape[0]
    d = w_a.shape[1]
    n_col = d // tn
    branch = lambda arr: pl.BlockSpec((tm, arr.shape[1]), lambda j, i: (i, 0))
    weight = lambda arr: pl.BlockSpec((arr.shape[0], tn), lambda j, i: (0, j), pipeline_mode=pl.Buffered(1))
    gate = lambda g: pl.BlockSpec((tm, tn), lambda j, i: (i, P_GATE * LANES // tn + g * n_col + j))
    return pl.pallas_call(
        _merge_kernel,
        grid=(n_col, t // tm),
        in_specs=[branch(o_a), branch(o_b), branch(o_c), weight(w_a), weight(w_b), weight(w_c),
                  gate(0), gate(1), gate(2)],
        out_specs=pl.BlockSpec((tm, tn), lambda j, i: (i, j)),
        out_shape=jax.ShapeDtypeStruct((t, d), BF16),
        compiler_params=_cparams("arbitrary", "arbitrary"),
        name="merge_branches",
    )(o_a, o_b, o_c, w_a, w_b, w_c, p, p, p)


def _residual_kernel(a_ref, w_ref, x_ref, g_ref, o_ref):
    o_ref[...] = x_ref[...] + g_ref[...] * jnp.dot(a_ref[...], w_ref[...], preferred_element_type=F32)


def matmul_gated_residual(a, w, x, gate, seq, *, tm=512, tn=512):
    t, k = a.shape
    d = w.shape[1]
    tm = min(tm, seq)
    tiles_per_batch = seq // tm
    return pl.pallas_call(
        _residual_kernel,
        grid=(t // tm, d // tn),
        in_specs=[
            pl.BlockSpec((tm, k), lambda i, j: (i, 0)),
            pl.BlockSpec((k, tn), lambda i, j: (0, j)),
            pl.BlockSpec((tm, tn), lambda i, j: (i, j)),
            pl.BlockSpec((None, 1, tn), lambda i, j: (i // tiles_per_batch, 0, j)),
        ],
        out_specs=pl.BlockSpec((tm, tn), lambda i, j: (i, j)),
        out_shape=jax.ShapeDtypeStruct((t, d), F32),
        compiler_params=_cparams("arbitrary", "arbitrary"),
        name="matmul_gated_residual",
    )(a, w, x, gate)


def _swiglu_up_kernel(x_ref, gain_ref, scale_ref, shift_ref, w1_ref, w3_ref, o_ref, h_ref):
    @pl.when(pl.program_id(1) == 0)
    def _():
        _store_norm_mod(x_ref, gain_ref, scale_ref, shift_ref, h_ref)

    h = h_ref[...]
    up = jnp.dot(h, w1_ref[...], preferred_element_type=F32)
    o_ref[...] = (_silu(up) * jnp.dot(h, w3_ref[...], preferred_element_type=F32)).astype(BF16)


def norm_swiglu_up(x, gain, scale, shift, w1, w3, seq, *, tm=1024, tn=512):
    t, d = x.shape
    f = w1.shape[1]
    tm = min(tm, seq)
    return pl.pallas_call(
        _swiglu_up_kernel,
        grid=(t // tm, f // tn),
        in_specs=_row_specs(d, tm, seq // tm) + [pl.BlockSpec((d, tn), lambda i, j: (0, j))] * 2,
        out_specs=pl.BlockSpec((tm, tn), lambda i, j: (i, j)),
        out_shape=jax.ShapeDtypeStruct((t, f), BF16),
        scratch_shapes=[pltpu.VMEM((tm, d), BF16)],
        compiler_params=_cparams("arbitrary", "arbitrary"),
        name="norm_swiglu_up",
    )(x, gain.reshape(1, d), scale, shift, w1, w3)


def _router_kernel(x_ref, gain_ref, scale_ref, shift_ref, w_ref, b_ref, h_ref, route_ref):
    h = _norm_mod(x_ref[...], gain_ref[...], scale_ref[...], shift_ref[...])
    h_ref[...] = h
    lane =lax.broadcasted_iota(jnp.int32, route_ref.shape, 1)
    logits = jnp.where(lane < N_EXPERTS, _fdot(h, w_ref[...]) + b_ref[...], -jnp.inf)
    top1 = jnp.max(logits, axis=-1, keepdims=True)
    idx1 = jnp.min(jnp.where(logits == top1, lane, LANES), axis=-1, keepdims=True)
    rest = jnp.where(lane == idx1, -jnp.inf, logits)
    top2 = jnp.max(rest, axis=-1, keepdims=True)
    idx2 = jnp.min(jnp.where(rest == top2, lane, LANES), axis=-1, keepdims=True)
    e2 = jnp.exp(top2 - top1)
    inv = 1.0 / (1.0 + e2)
    route_ref[...] = jnp.where(lane == 0, idx1.astype(F32),
                               jnp.where(lane == 1, idx2.astype(F32),
                                         jnp.where(lane == 2, inv, jnp.where(lane == 3, e2 * inv, 0.0))))


def norm_router(x, gain, scale, shift, w_router, b_router, seq, *, tm=512):
    t, d = x.shape
    w_pad = jnp.zeros((d, LANES), F32).at[:, :N_EXPERTS].set(w_router)
    b_pad = jnp.zeros((1, LANES), F32).at[0, :N_EXPERTS].set(b_router)
    tiles_per_batch = seq // tm
    return pl.pallas_call(
        _router_kernel,
        grid=(t // tm,),
        in_specs=[
            pl.BlockSpec((tm, d), lambda i: (i, 0)),
            pl.BlockSpec((1, d), lambda i: (0, 0)),
            pl.BlockSpec((None, 1, d), lambda i: (i // tiles_per_batch, 0, 0)),
            pl.BlockSpec((None, 1, d), lambda i: (i // tiles_per_batch, 0, 0)),
            pl.BlockSpec((d, LANES), lambda i: (0, 0)),
            pl.BlockSpec((1, LANES), lambda i: (0, 0)),
        ],
        out_specs=[pl.BlockSpec((tm, d), lambda i: (i, 0)), pl.BlockSpec((tm, LANES), lambda i: (i, 0))],
        out_shape=[jax.ShapeDtypeStruct((t, d), F32), jax.ShapeDtypeStruct((t, LANES), F32)],
        compiler_params=_cparams("arbitrary"),
        name="norm_router",
    )(x, gain.reshape(1, d), scale, shift, w_pad, b_pad)


def _row_copy(src_hbm, buf_ref, slot, r, src_row, sem_rows):
    return pltpu.make_async_copy(src_hbm.at[pl.ds(src_row, 1)], buf_ref.at[slot, pl.ds(r, 1)], sem_rows.at[slot])


def _start_row_gather(src_hbm, jobs, slot, sem_idx, sem_rows):
    for idx_ref, idx_smem, buf_ref in jobs:
        load = pltpu.make_async_copy(idx_ref, idx_smem.at[slot], sem_idx)
        load.start()
        load.wait()

        def issue(r, carry, idx_smem=idx_smem, buf_ref=buf_ref):
            _row_copy(src_hbm, buf_ref, slot, r, idx_smem[slot, 0, 0, r], sem_rows).start()
            return carry

        lax.fori_loop(0, buf_ref.shape[1], issue, 0, unroll=8)


def _wait_row_gather(src_hbm, jobs, slot, sem_rows):
    for _, _, buf_ref in jobs:
        def drain(r, carry, buf_ref=buf_ref):
            _row_copy(src_hbm, buf_ref, slot, r, 0, sem_rows).wait()
            return carry

        lax.fori_loop(0, buf_ref.shape[1], drain, 0, unroll=8)


def _pipelined_row_gather(step, n_live, src_hbm, jobs_now, jobs_next, sem_idx, sem_rows):
    slot = step % 2

    @pl.when(step == 0)
    def _():
        _start_row_gather(src_hbm, jobs_now, 0, sem_idx, sem_rows)

    @pl.when(step + 1 < n_live)
    def _():
        _start_row_gather(src_hbm, jobs_next, 1 - slot, sem_idx, sem_rows)

    _wait_row_gather(src_hbm, jobs_now, slot, sem_rows)
    return slot


def _gather_scratch(rows, d, n_jobs):
    return ([pltpu.VMEM((2, rows, d), F32)] * n_jobs + [pltpu.SMEM((2, 1, 1, rows), jnp.int32)] * n_jobs
            + [pltpu.SemaphoreType.DMA(()), pltpu.SemaphoreType.DMA((2,))])


def _dispatch_kernel(used_ref, idx_ref, idx_next_ref, h_hbm, o_ref, buf_ref, idx_smem, sem_idx, sem_rows):
    step = pl.program_id(0)

    @pl.when(step < used_ref[0])
    def _():
        slot = _pipelined_row_gather(step, used_ref[0], h_hbm, [(idx_ref, idx_smem, buf_ref)],
                                     [(idx_next_ref, idx_smem, buf_ref)], sem_idx, sem_rows)
        o_ref[...] = buf_ref[slot].astype(BF16)

    @pl.when(step >= used_ref[0])
    def _():
        o_ref[...] = jnp.zeros(o_ref.shape, BF16)


def expert_dispatch(n_used, row_tok, h):
    d = h.shape[1]
    n_blocks = row_tok.shape[0] // MOE_BLOCK
    table = row_tok.reshape(n_blocks, 1, MOE_BLOCK)
    return pl.pallas_call(
        _dispatch_kernel,
        grid_spec=pltpu.PrefetchScalarGridSpec(
            num_scalar_prefetch=1,
            grid=(n_blocks,),
            in_specs=[
                pl.BlockSpec((1, 1, MOE_BLOCK), lambda i, used: (i, 0, 0)),
                pl.BlockSpec((1, 1, MOE_BLOCK), lambda i, used: (jnp.minimum(i + 1, n_blocks - 1), 0, 0)),
                pl.BlockSpec(memory_space=pl.ANY),
            ],
            out_specs=pl.BlockSpec((MOE_BLOCK, d), lambda i, used: (i, 0)),
            scratch_shapes=_gather_scratch(MOE_BLOCK, d, 1),
        ),
        out_shape=jax.ShapeDtypeStruct((n_blocks * MOE_BLOCK, d), BF16),
        compiler_params=_cparams("arbitrary"),
        name="expert_dispatch",
    )(n_used, table, table, h)


def _expert_up_kernel(be_ref, used_ref, x_ref, w1_ref, w3_ref, o_ref, w1b_ref, w3b_ref):
    i = pl.program_id(1)

    @pl.when(jnp.logical_or(i == 0, be_ref[i] != be_ref[jnp.maximum(i - 1, 0)]))
    def _():
        w1b_ref[...] = w1_ref[...].astype(BF16)
        w3b_ref[...] = w3_ref[...].astype(BF16)

    @pl.when(i < used_ref[0])
    def _():
        x = x_ref[...]
        up = jnp.dot(x, w1b_ref[...], preferred_element_type=F32)
        o_ref[...] = (_silu(up) * jnp.dot(x, w3b_ref[...], preferred_element_type=F32)).astype(BF16)

    @pl.when(i >= used_ref[0])
    def _():
        o_ref[...] = jnp.zeros(o_ref.shape, BF16)


def expert_swiglu_up(block_e, n_used, xs, w1, w3, *, tn=512):
    rows, d = xs.shape
    f = w1.shape[2]
    return pl.pallas_call(
        _expert_up_kernel,
        grid_spec=pltpu.PrefetchScalarGridSpec(
            num_scalar_prefetch=2,
            grid=(f // tn, rows // MOE_BLOCK),
            in_specs=[
                pl.BlockSpec((MOE_BLOCK, d), lambda j, i, be, used: (i, 0)),
                pl.BlockSpec((None, d, tn), lambda j, i, be, used: (be[i], 0, j)),
                pl.BlockSpec((None, d, tn), lambda j, i, be, used: (be[i], 0, j)),
            ],
            out_specs=pl.BlockSpec((MOE_BLOCK, tn), lambda j, i, be, used: (i, j)),
            scratch_shapes=[pltpu.VMEM((d, tn), BF16), pltpu.VMEM((d, tn), BF16)],
        ),
        out_shape=jax.ShapeDtypeStruct((rows, f), BF16),
        compiler_params=_cparams("arbitrary", "arbitrary"),
        name="expert_swiglu_up",
    )(block_e, n_used, xs, w1, w3)


def _expert_down_kernel(be_ref, used_ref, a_ref, w_ref, o_ref):
    i = pl.program_id(1)

    @pl.when(i < used_ref[0])
    def _():
        o_ref[...] = jnp.dot(a_ref[...], w_ref[...], preferred_element_type=F32)

    @pl.when(i >= used_ref[0])
    def _():
        o_ref[...] = jnp.zeros(o_ref.shape, F32)


def expert_down(block_e, n_used, act, w2, *, tn=1024):
    rows, f = act.shape
    d = w2.shape[2]
    return pl.pallas_call(
        _expert_down_kernel,
        grid_spec=pltpu.PrefetchScalarGridSpec(
            num_scalar_prefetch=2,
            grid=(d // tn, rows // MOE_BLOCK),
            in_specs=[
                pl.BlockSpec((MOE_BLOCK, f), lambda j, i, be, used: (i, 0)),
                pl.BlockSpec((None, f, tn), lambda j, i, be, used: (be[i], 0, j)),
            ],
            out_specs=pl.BlockSpec((MOE_BLOCK, tn), lambda j, i, be, used: (i, j)),
        ),
        out_shape=jax.ShapeDtypeStruct((rows, d), F32),
        compiler_params=_cparams("arbitrary", "arbitrary"),
        name="expert_down",
    )(block_e, n_used, act, w2)


def _combine_kernel(idx1_ref, idx2_ref, idx1_next_ref, idx2_next_ref, x_ref, route_ref, g_ref, fin_ref, y_hbm,
                    o_ref, buf1_ref, buf2_ref, idx1_smem, idx2_smem, sem_idx, sem_rows, *, final):
    slot = _pipelined_row_gather(
        pl.program_id(0), pl.num_programs(0), y_hbm,
        [(idx1_ref, idx1_smem, buf1_ref), (idx2_ref, idx2_smem, buf2_ref)],
        [(idx1_next_ref, idx1_smem, buf1_ref), (idx2_next_ref, idx2_smem, buf2_ref)], sem_idx, sem_rows)
    route = route_ref[...]
    y = route[:, 2:3] * buf1_ref[slot] + route[:, 3:4] * buf2_ref[slot]
    out = x_ref[...] + g_ref[...] * y
    o_ref[...] = _rms(out, fin_ref[...]) if final else out


def combine_residual(x, y, dest, route, gate, seq, final_gain=None, *, tm=256):
    t, d = x.shape
    final = final_gain is not None
    fin = (final_gain if final else jnp.ones((d,), F32)).reshape(1, d)
    tiles_per_batch = seq // tm
    n_tiles = t // tm
    row = pl.BlockSpec((tm, d), lambda i: (i, 0))
    idx = pl.BlockSpec((1, 1, tm), lambda i: (i, 0, 0))
    idx_next = pl.BlockSpec((1, 1, tm), lambda i: (jnp.minimum(i + 1, n_tiles - 1), 0, 0))
    tables = [dest[:, k].reshape(n_tiles, 1, tm) for k in range(TOP_K)]
    return pl.pallas_call(
        functools.partial(_combine_kernel, final=final),
        grid=(n_tiles,),
        in_specs=[idx, idx, idx_next, idx_next, row, pl.BlockSpec((tm, LANES), lambda i: (i, 0)),
                  pl.BlockSpec((None, 1, d), lambda i: (i // tiles_per_batch, 0, 0)),
                  pl.BlockSpec((1, d), lambda i: (0, 0)), pl.BlockSpec(memory_space=pl.ANY)],
        out_specs=row,
        out_shape=jax.ShapeDtypeStruct((t, d), F32),
        scratch_shapes=_gather_scratch(tm, d, TOP_K),
        compiler_params=_cparams("arbitrary"),
        name="combine_residual",
    )(*tables, *tables, x, route, gate, fin, y)


def routing_plan(route, n_tokens):
    n_pairs = n_tokens * TOP_K
    flat_e = route[:, :TOP_K].astype(jnp.int32).reshape(n_pairs)
    onehot = (flat_e[:, None] == jnp.arange(N_EXPERTS, dtype=jnp.int32)[None, :]).astype(jnp.int32)
    running = jnp.cumsum(onehot, axis=0)
    counts = running[-1]
    rank = jnp.take_along_axis(running, flat_e[:, None], axis=1)[:, 0] - 1
    padded = (counts + MOE_BLOCK - 1) // MOE_BLOCK * MOE_BLOCK
    pend = jnp.cumsum(padded)
    dest = (pend - padded)[flat_e] + rank
    n_blocks = -(-n_pairs // MOE_BLOCK) + N_EXPERTS
    flat_tok = jnp.repeat(jnp.arange(n_tokens, dtype=jnp.int32), TOP_K)
    row_tok = jnp.zeros((n_blocks * MOE_BLOCK,), jnp.int32).at[dest].set(flat_tok)
    block_e = jnp.minimum(jnp.searchsorted(pend, jnp.arange(n_blocks, dtype=jnp.int32) * MOE_BLOCK, side="right"),
                          N_EXPERTS - 1).astype(jnp.int32)
    n_used = (pend[-1:] // MOE_BLOCK).astype(jnp.int32)
    return row_tok, dest.reshape(n_tokens, TOP_K), block_e, n_used


def _final_norm_kernel(x_ref, gain_ref, o_ref):
    o_ref[...] = _rms(x_ref[...], gain_ref[...])


def final_rms_norm(x, gain, *, tm=512):
    t, d = x.shape
    return pl.pallas_call(
        _final_norm_kernel,
        grid=(t // tm,),
        in_specs=[pl.BlockSpec((tm, d), lambda i: (i, 0)), pl.BlockSpec((1, d), lambda i: (0, 0))],
        out_specs=pl.BlockSpec((tm, d), lambda i: (i, 0)),
        out_shape=jax.ShapeDtypeStruct((t, d), F32),
        compiler_params=_cparams("arbitrary"),
        name="final_norm",
    )(x, gain.reshape(1, d))


def _pad_cols(w, width):
    return jnp.pad(w, ((0, 0), (0, width - w.shape[1])))


def projection_weight(w_in, w_gate):
    lru_width = (w_in.shape[1] - IN_FIXED_WIDTH) // 2
    o = np.cumsum((0,) + IN_FIXED_WIDTHS + (lru_width, lru_width))
    c_q, c_kv, k_rope, qkv, z, ab, lru_x, lru_y = (w_in[:, int(o[k]):int(o[k + 1])] for k in range(8))
    parts = [w_gate, qkv, z, lru_x, lru_y, c_q, c_kv, _pad_cols(k_rope, LANES), _pad_cols(ab, LANES)]
    return jnp.concatenate(parts, axis=1).astype(BF16)


def mla_weights(w_uq, w_ukv):
    d_q, d_kv = w_uq.shape[0], w_ukv.shape[0]
    wq = w_uq.reshape(d_q, MLA_HEADS, MLA_NOPE + MLA_ROPE)
    wq = jnp.pad(wq, ((0, 0), (0, 0), (0, MLA_QK_PAD - MLA_NOPE - MLA_ROPE))).reshape(d_q, MLA_HEADS * MLA_QK_PAD)
    wkv = w_ukv.reshape(d_kv, MLA_HEADS, MLA_NOPE + MLA_V)
    wk = wkv[:, :, :MLA_NOPE].reshape(d_kv, MLA_HEADS * MLA_NOPE)
    wv = wkv[:, :, MLA_NOPE:].reshape(d_kv, MLA_HEADS * MLA_V)
    return wq.astype(BF16), wk.astype(BF16), wv.astype(BF16)


def _gdn_gate_rows(a_log, dt_bias):
    zeros = jnp.zeros((GDN_HEADS,), F32)
    row = lambda v: _pad_cols(jnp.concatenate([v[0], zeros, v[1], zeros])[None, :], LANES)
    return row(a_log.astype(F32)), row(dt_bias.astype(F32))


def kernel(x, c, positions, ada_w, ada_b, norm_mix, norm_ffn, w_in, mla_q_norm, mla_w_uq, mla_kv_norm,
           mla_w_ukv, gdn_conv, gdn_a_log, gdn_dt_bias, gdn_out_norm, lru_conv_w, lru_conv_b, lru_w_a,
           lru_b_a, lru_w_i, lru_b_i, lru_lambda, w_branch_a, w_branch_b, w_branch_c, w_gate, w_out,
           ffn_w1, ffn_w3, ffn_w2, moe_router, moe_router_b, moe_w1, moe_w3, moe_w2, final_norm):
    bsz, seq, d = x.shape
    t = bsz * seq
    depth = ada_w.shape[0]
    xf = x.reshape(t, d)
    mod = ada_modulation(c, ada_w, ada_b).reshape(depth, bsz, 6, 1, d)
    cos, sin = rope_tables(positions)
    for layer in range(depth):
        shift_m, scale_m, gate_m, shift_f, scale_f, gate_f = (mod[layer, :, k] for k in range(6))
        p = norm_projection(xf, norm_mix[layer], scale_m, shift_m, projection_weight(w_in[layer], w_gate[layer]), seq)
        wq, wk, wv = mla_weights(mla_w_uq[layer], mla_w_ukv[layer])
        q = mla_q(p, mla_q_norm[layer], wq, cos, sin)
        k, v = mla_kv(p, mla_kv_norm[layer], wk, wv, cos, sin)
        o_a = flash_attention(q, k, v, bsz, seq)
        qkv = gdn_prep(p, gdn_conv[layer], seq)
        a_log_row, dt_row = _gdn_gate_rows(gdn_a_log[layer], gdn_dt_bias[layer])
        o_f, o_r = gdn_chunk_scan(qkv, p, a_log_row, dt_row, bsz, seq)
        o_b = gdn_output(o_f, o_r, p, gdn_out_norm[layer])
        o_c = rg_lru(p, lru_conv_w[layer], lru_conv_b[layer], lru_w_a[layer], lru_b_a[layer], lru_w_i[layer],
                     lru_b_i[layer], lru_lambda[layer], seq)
        merged = merge_branches(o_a, o_b, o_c, w_branch_a[layer].astype(BF16), w_branch_b[layer].astype(BF16),
                                w_branch_c[layer].astype(BF16), p)
        xf = matmul_gated_residual(merged, w_out[layer].astype(BF16), xf, gate_m, seq, tm=512, tn=d)
        j = layer // 2
        if layer % 2 == 0:
            act = norm_swiglu_up(xf, norm_ffn[layer], scale_f, shift_f, ffn_w1[j].astype(BF16),
                                 ffn_w3[j].astype(BF16), seq)
            xf = matmul_gated_residual(act, ffn_w2[j].astype(BF16), xf, gate_f, seq, tm=1024, tn=256)
        else:
            h, route = norm_router(xf, norm_ffn[layer], scale_f, shift_f, moe_router[j], moe_router_b[j], seq)
            row_tok, dest, block_e, n_used = routing_plan(route, t)
            act = expert_swiglu_up(block_e, n_used, expert_dispatch(n_used, row_tok, h), moe_w1[j], moe_w3[j])
            y = expert_down(block_e, n_used, act, moe_w2[j].astype(BF16))
            closing = layer == depth - 1
            xf = combine_residual(xf, y, dest, route, gate_f, seq, final_norm if closing else None)
    if depth % 2:
        xf = final_rms_norm(xf, final_norm)
    return xf.reshape(bsz, seq, d)
```
